```python
import jax, jax.numpy as jnp
from jax import lax
import numpy as np

D_MODEL = 1024
BATCH = 8
SEQ = 8192
DEPTH = 1

CHUNK = 64
NORM_EPS = 1e-6

GDN_HEADS = 8
GDN_DK = 64
GDN_DV = 64
GDN_QK_W = GDN_HEADS * GDN_DK
GDN_V_W = GDN_HEADS * GDN_DV
GDN_QKV_W = 2 * GDN_QK_W + GDN_V_W
CONV_K = 4

MLA_HEADS = 8
MLA_NOPE = 64
MLA_ROPE = 32
MLA_QK = MLA_NOPE + MLA_ROPE
MLA_V = 64
MLA_V_W = MLA_HEADS * MLA_V
Q_LORA = 384
KV_LORA = 256
ROPE_THETA = 10000.0
Q_BLOCK = 128
MAX_POS_OFFSET = 4096

PEER_HEADS = 8
PEER_DKEY = 256
N_KEYS = 128
N_EXPERTS = N_KEYS * N_KEYS
PEER_TOPK = 16
PEER_TOKEN_BLOCK = 128

IN_SPLITS = (GDN_QKV_W, GDN_V_W, GDN_HEADS, GDN_HEADS, Q_LORA, KV_LORA, MLA_ROPE, 2 * D_MODEL)
IN_TOTAL = sum(IN_SPLITS)

kernel_name = "chunk_causal_gdn_mla_peer_block"


def rms_norm(x, g):
    xf = x.astype(jnp.float32)
    y = xf * lax.rsqrt(jnp.mean(xf * xf, axis=-1, keepdims=True) + NORM_EPS)
    return (y * g.astype(jnp.float32)).astype(x.dtype)


def l2_norm(x):
    xf = x.astype(jnp.float32)
    return xf * lax.rsqrt(jnp.sum(xf * xf, axis=-1, keepdims=True) + NORM_EPS)


def causal_depthwise_conv(x, w):
    c = x.shape[-1]
    return lax.conv_general_dilated(
        x, w[:, None, :].astype(x.dtype), window_strides=(1,), padding=[(CONV_K - 1, 0)],
        dimension_numbers=("NWC", "WIO", "NWC"), feature_group_count=c)


def rope(x, pos):
    half = MLA_ROPE // 2
    inv = jnp.power(ROPE_THETA, -jnp.arange(half, dtype=jnp.float32) / half)
    ang = pos.astype(jnp.float32)[:, :, None, None] * inv
    cos, sin = jnp.cos(ang), jnp.sin(ang)
    xf = x.astype(jnp.float32)
    x1, x2 = xf[..., :half], xf[..., half:]
    return jnp.concatenate([x1 * cos - x2 * sin, x2 * cos + x1 * sin], axis=-1).astype(x.dtype)


def gated_delta_rule(q, k, v, g, beta):
    b, s, h, dk = q.shape
    dv = v.shape[-1]
    nc = s // CHUNK
    f32 = jnp.float32

    def to_chunks(t):
        return t.astype(f32).reshape(b, nc, CHUNK, h, -1).transpose(0, 3, 1, 2, 4)

    q, k, v = to_chunks(q), to_chunks(k), to_chunks(v)
    g = g.astype(f32).reshape(b, nc, CHUNK, h).transpose(0, 3, 1, 2)
    beta = beta.astype(f32).reshape(b, nc, CHUNK, h).transpose(0, 3, 1, 2)
    gc = jnp.cumsum(g, axis=-1)
    idx = jnp.arange(CHUNK)
    strict = idx[:, None] > idx[None, :]
    causal = idx[:, None] >= idx[None, :]
    decay = jnp.exp(jnp.where(causal, gc[..., :, None] - gc[..., None, :], -jnp.inf))

    kb = k * beta[..., None]
    lower = jnp.where(strict, jnp.einsum("bhncd,bhnsd->bhncs", kb, k) * decay, 0.0)
    eye = jnp.eye(CHUNK, dtype=f32)
    t_inv = lax.linalg.triangular_solve(eye + lower, jnp.broadcast_to(eye, lower.shape),
                                        left_side=True, lower=True)
    u = jnp.einsum("bhncs,bhnsd->bhncd", t_inv, v * beta[..., None])
    w = jnp.einsum("bhncs,bhnsd->bhncd", t_inv, kb * jnp.exp(gc)[..., None])
    attn = jnp.einsum("bhncd,bhnsd->bhncs", q, k) * decay
    qg = q * jnp.exp(gc)[..., None]
    kdec = k * jnp.exp(gc[..., -1:] - gc)[..., None]
    glast = jnp.exp(gc[..., -1])

    def step(state, xs):
        u_c, w_c, a_c, qg_c, kd_c, gl_c = xs
        v_new = u_c - jnp.einsum("bhcd,bhde->bhce", w_c, state)
        o = jnp.einsum("bhcd,bhde->bhce", qg_c, state) + jnp.einsum("bhcs,bhse->bhce", a_c, v_new)
        state = state * gl_c[..., None, None] + jnp.einsum("bhcd,bhce->bhde", kd_c, v_new)
        return state, o

    xs = tuple(jnp.moveaxis(t, 2, 0) for t in (u, w, attn, qg, kdec, glast))
    _, o = lax.scan(step, jnp.zeros((b, h, dk, dv), f32), xs)
    return o.transpose(1, 0, 3, 2, 4).reshape(b, s, h, dv)


def chunk_causal_attention(q, k, v):
    b, s, h, dqk = q.shape
    dv = v.shape[-1]
    nqb = s // Q_BLOCK
    scale = dqk ** -0.5
    qb = q.astype(jnp.float32).reshape(b, nqb, Q_BLOCK, h, dqk).transpose(1, 0, 3, 2, 4)
    kh = k.astype(jnp.float32).transpose(0, 2, 1, 3)
    vh = v.astype(jnp.float32).transpose(0, 2, 1, 3)
    key_chunk = jnp.arange(s) // CHUNK

    def one_block(args):
        q_blk, start = args
        sc = jnp.einsum("bhqd,bhkd->bhqk", q_blk, kh) * scale
        q_chunk = (start + jnp.arange(Q_BLOCK)) // CHUNK
        sc = jnp.where(key_chunk[None, :] <= q_chunk[:, None], sc, -jnp.inf)
        p = jax.nn.softmax(sc, axis=-1)
        return jnp.einsum("bhqk,bhkd->bhqd", p, vh)

    o = lax.map(one_block, (qb, jnp.arange(nqb) * Q_BLOCK))
    return o.transpose(1, 0, 3, 2, 4).reshape(b, s, h * dv).astype(q.dtype)


def peer_ffn(h, w_pq, sub_keys, u_tab, v_tab):
    b, s, d = h.shape
    q = (h @ w_pq).reshape(b, s, PEER_HEADS, 2, PEER_DKEY // 2)
    scores = jnp.einsum("bshpd,hpnd->bshpn", q.astype(jnp.float32), sub_keys.astype(jnp.float32))
    top_s, top_i = lax.top_k(scores, PEER_TOPK)
    cand_s = (top_s[..., 0, :, None] + top_s[..., 1, None, :]).reshape(b, s, PEER_HEADS, PEER_TOPK * PEER_TOPK)
    cand_i = (top_i[..., 0, :, None] * N_KEYS + top_i[..., 1, None, :]).reshape(b, s, PEER_HEADS, PEER_TOPK * PEER_TOPK)
    best_s, best_pos = lax.top_k(cand_s, PEER_TOPK)
    expert_idx = jnp.take_along_axis(cand_i, best_pos, axis=-1)
    gate = jax.nn.softmax(best_s, axis=-1)

    nb = (b * s) // PEER_TOKEN_BLOCK
    hb = h.reshape(nb, PEER_TOKEN_BLOCK, d)
    ib = expert_idx.reshape(nb, PEER_TOKEN_BLOCK, PEER_HEADS, PEER_TOPK)
    gb = gate.reshape(nb, PEER_TOKEN_BLOCK, PEER_HEADS, PEER_TOPK)

    def block(args):
        h_blk, i_blk, g_blk = args
        u = u_tab[i_blk]
        act = jax.nn.gelu(jnp.einsum("td,thkd->thk", h_blk, u).astype(jnp.float32), approximate=False)
        vv = v_tab[i_blk]
        return jnp.einsum("thk,thkd->td", (g_blk * act).astype(vv.dtype), vv)

    y = lax.map(block, (hb, ib, gb))
    return y.reshape(b, s, d).astype(h.dtype)


def setup_inputs(seed: int = 0) -> dict:
    key = jax.random.key(seed)
    ks = jax.random.split(key, 24)
    f32 = jnp.float32
    L = DEPTH

    def nrm(k, shape, scale):
        return jax.random.normal(k, shape, f32) * scale

    def gain(k, n):
        return 1.0 + 0.02 * jax.random.normal(k, (L, n), f32)

    x = jax.random.normal(ks[0], (BATCH, SEQ, D_MODEL), f32)
    offset = jax.random.randint(ks[1], (BATCH, 1), 0, MAX_POS_OFFSET // CHUNK) * CHUNK
    positions = (offset + jnp.arange(SEQ, dtype=jnp.int32)[None, :]).astype(jnp.int32)
    a_log = jnp.log(jax.random.uniform(ks[5], (L, GDN_HEADS), f32, 1.0, 16.0))
    dt = jnp.exp(jax.random.uniform(ks[6], (L, GDN_HEADS), f32, np.log(1e-3), np.log(1e-1)))
    dt_bias = dt + jnp.log(-jnp.expm1(-dt))
    return {
        "x": x,
        "positions": positions,
        "norm1_g": gain(ks[2], D_MODEL),
        "w_in": nrm(ks[3], (L, D_MODEL, IN_TOTAL), D_MODEL ** -0.5),
        "conv_w": nrm(ks[4], (L, CONV_K, GDN_QKV_W), CONV_K ** -0.5),
        "a_log": a_log,
        "dt_bias": dt_bias,
        "gdn_norm_g": gain(ks[7], GDN_DV),
        "w_gdn_out": nrm(ks[8], (L, GDN_V_W, D_MODEL), GDN_V_W ** -0.5),
        "cq_norm_g": gain(ks[9], Q_LORA),
        "w_uq": nrm(ks[10], (L, Q_LORA, MLA_HEADS * MLA_QK), Q_LORA ** -0.5),
        "ckv_norm_g": gain(ks[11], KV_LORA),
        "w_ukv": nrm(ks[12], (L, KV_LORA, MLA_HEADS * (MLA_NOPE + MLA_V)), KV_LORA ** -0.5),
        "q_norm_g": gain(ks[13], MLA_QK),
        "k_norm_g": gain(ks[14], MLA_QK),
        "w_mla_out": nrm(ks[15], (L, MLA_V_W, D_MODEL), MLA_V_W ** -0.5),
        "w_o": nrm(ks[16], (L, D_MODEL, D_MODEL), D_MODEL ** -0.5),
        "norm2_g": gain(ks[17], D_MODEL),
        "w_pq": nrm(ks[18], (L, D_MODEL, PEER_HEADS * PEER_DKEY), D_MODEL ** -0.5),
        "sub_keys": nrm(ks[19], (L, PEER_HEADS, 2, N_KEYS, PEER_DKEY // 2), (PEER_DKEY // 2) ** -0.5),
        "u_tab": nrm(ks[20], (L, N_EXPERTS, D_MODEL), D_MODEL ** -0.5),
        "v_tab": nrm(ks[21], (L, N_EXPERTS, D_MODEL), 0.5),
    }


def reference(x, positions, norm1_g, w_in, conv_w, a_log, dt_bias, gdn_norm_g, w_gdn_out,
              cq_norm_g, w_uq, ckv_norm_g, w_ukv, q_norm_g, k_norm_g, w_mla_out,
              w_o, norm2_g, w_pq, sub_keys, u_tab, v_tab):
    b, s, d = x.shape
    split_points = np.cumsum(IN_SPLITS)[:-1].tolist()
    for l in range(DEPTH):
        h = rms_norm(x, norm1_g[l])
        proj = h @ w_in[l]
        qkv, z, b_raw, a_raw, cq, ckv, k_rot, gates = jnp.split(proj, split_points, axis=-1)

        qkv = jax.nn.silu(causal_depthwise_conv(qkv, conv_w[l]))
        gq, gk, gv = jnp.split(qkv, [GDN_QK_W, 2 * GDN_QK_W], axis=-1)
        gq = l2_norm(gq.reshape(b, s, GDN_HEADS, GDN_DK)) * (GDN_DK ** -0.5)
        gk = l2_norm(gk.reshape(b, s, GDN_HEADS, GDN_DK))
        gv = gv.reshape(b, s, GDN_HEADS, GDN_DV)
        beta = jax.nn.sigmoid(b_raw.astype(jnp.float32))
        log_decay = -jnp.exp(a_log[l].astype(jnp.float32)) * jax.nn.softplus(
            a_raw.astype(jnp.float32) + dt_bias[l].astype(jnp.float32))
        o_a = gated_delta_rule(gq, gk, gv, log_decay, beta).astype(x.dtype)
        o_a = rms_norm(o_a, gdn_norm_g[l]) * jax.nn.silu(z.reshape(b, s, GDN_HEADS, GDN_DV))
        y_a = o_a.reshape(b, s, GDN_V_W) @ w_gdn_out[l]

        mq = (rms_norm(cq, cq_norm_g[l]) @ w_uq[l]).reshape(b, s, MLA_HEADS, MLA_QK)
        kv = (rms_norm(ckv, ckv_norm_g[l]) @ w_ukv[l]).reshape(b, s, MLA_HEADS, MLA_NOPE + MLA_V)
        k_nope, mv = kv[..., :MLA_NOPE], kv[..., MLA_NOPE:]
        k_rot_h = jnp.broadcast_to(k_rot[:, :, None, :], (b, s, MLA_HEADS, MLA_ROPE))
        mk = jnp.concatenate([k_nope, k_rot_h], axis=-1)
        mq = rms_norm(mq, q_norm_g[l])
        mk = rms_norm(mk, k_norm_g[l])
        mq = jnp.concatenate([mq[..., :MLA_NOPE], rope(mq[..., MLA_NOPE:], positions)], axis=-1)
        mk = jnp.concatenate([mk[..., :MLA_NOPE], rope(mk[..., MLA_NOPE:], positions)], axis=-1)
        y_b = chunk_causal_attention(mq, mk, mv) @ w_mla_out[l]

        g_a, g_b = jnp.split(gates, 2, axis=-1)
        merged = jax.nn.sigmoid(g_a) * y_a + jax.nn.sigmoid(g_b) * y_b
        x = x + merged @ w_o[l]

        h2 = rms_norm(x, norm2_g[l])
        x = x + peer_ffn(h2, w_pq[l], sub_keys[l], u_tab[l], v_tab[l])
    return x
```

```python
import functools

import jax
import jax.numpy as jnp
import numpy as np
from jax import lax
from jax.experimental import pallas as pl
from jax.experimental.pallas import tpu as pltpu

F32 = jnp.float32
BF16 = jnp.bfloat16

CHUNK = 64
NORM_EPS = 1e-6
ROPE_THETA = 10000.0
PEER_TOPK = 16
LANES = 128
VMEM_LIMIT = 56 * 1024 * 1024

NT_DIMS = (((1,), (1,)), ((), ()))
TN_DIMS = (((0,), (0,)), ((), ()))


def _dot(a, b):
    return jnp.dot(a, b, preferred_element_type=F32)


def _dot_nt(a, b):
    return lax.dot_general(a, b, NT_DIMS, preferred_element_type=F32)


def _dot_tn(a, b):
    return lax.dot_general(a, b, TN_DIMS, preferred_element_type=F32)


def _rms(v, g):
    ms = jnp.mean(v * v, axis=-1, keepdims=True)
    return v * lax.rsqrt(ms + NORM_EPS) * g


def _sigmoid(v):
    return 1.0 / (1.0 + jnp.exp(-v))


def _softplus(v):
    return jnp.maximum(v, 0.0) + jnp.log(1.0 + jnp.exp(-jnp.abs(v)))


def _full(shape):
    n = len(shape)
    return pl.BlockSpec(shape, lambda *_: (0,) * n)


def _inproj_body(x_ref, xprev_ref, pos_ref, n1g_ref, wqkv_ref, convw_ref, seg_ref, wz_ref,
                 wbg_ref, bgpar_ref, wcq_ref, cqg_ref, wuq_ref, qng_ref, wckv_ref, ckvg_ref,
                 wuk_ref, wuvt_ref, wkrot_ref, kng_ref, ropepar_ref, wg_ref,
                 gq_ref, gk_ref, gv_ref, sz_ref, bgc_ref, sg_ref, qp_ref, kp_ref, vt_ref,
                 *, tm, qk_w, dk, n_heads, d_rot_lo, d_rot_half, d_qk, dv_mla, conv_k):
    s = pl.program_id(1)
    n1g = n1g_ref[...]
    h = _rms(x_ref[0], n1g).astype(BF16)
    hp = _rms(xprev_ref[0, 0], n1g).astype(BF16)

    pre = _dot(h, wqkv_ref[...])
    pre_prev = _dot(hp, wqkv_ref[...])
    pre_prev = jnp.where(s == 0, 0.0, pre_prev)
    ext = jnp.concatenate([pre_prev, pre], axis=0)
    cw = convw_ref[...]
    conv = cw[0:1, :] * ext[8 - conv_k + 1:8 - conv_k + 1 + tm, :]
    for k in range(1, conv_k):
        off = 8 - conv_k + 1 + k
        conv = conv + cw[k:k + 1, :] * ext[off:off + tm, :]
    act = conv * _sigmoid(conv)
    seg = seg_ref[...]

    def l2n(t):
        ss = _dot((t * t).astype(BF16), seg)
        return t * lax.rsqrt(ss + NORM_EPS)

    gq_ref[...] = (l2n(act[:, :qk_w]) * (dk ** -0.5)).astype(BF16)
    gk_ref[...] = l2n(act[:, qk_w:2 * qk_w]).astype(BF16)
    gv_ref[...] = act[:, 2 * qk_w:].astype(BF16)

    z = _dot(h, wz_ref[...])
    sz_ref[...] = (z * _sigmoid(z)).astype(BF16)

    raw = _dot(h, wbg_ref[...])
    lane = lax.broadcasted_iota(jnp.int32, raw.shape, 1)
    g = bgpar_ref[0:1, :] * _softplus(raw + bgpar_ref[1:2, :])
    bgc_ref[...] = jnp.where(lane < n_heads, _sigmoid(raw), g)

    pos = pos_ref[0].astype(F32)
    ang = pos * ropepar_ref[0:1, :]
    cosv = jnp.cos(ang)
    sinv = jnp.sin(ang) * ropepar_ref[1:2, :]
    lane128 = lax.broadcasted_iota(jnp.int32, ang.shape, 1)
    first_half = lane128 < d_rot_lo + d_rot_half

    def rope(y):
        partner = jnp.where(first_half, pltpu.roll(y, LANES - d_rot_half, 1), pltpu.roll(y, d_rot_half, 1))
        return y * cosv + partner * sinv

    def head_norm(slab, gain):
        ms = jnp.sum(slab * slab, axis=-1, keepdims=True) * (1.0 / d_qk)
        return slab * lax.rsqrt(ms + NORM_EPS) * gain

    cqn = _rms(_dot(h, wcq_ref[...]), cqg_ref[...]).astype(BF16)
    qall = _dot(cqn, wuq_ref[...])
    ckvn = _rms(_dot(h, wckv_ref[...]), ckvg_ref[...]).astype(BF16)
    kall = _dot(ckvn, wuk_ref[...])
    krot = _dot(h, wkrot_ref[...])
    qng = qng_ref[...]
    kng = kng_ref[...]
    for hh in range(n_heads):
        sl = slice(hh * LANES, (hh + 1) * LANES)
        qp_ref[0, hh] = rope(head_norm(qall[:, sl], qng)).astype(BF16)
        kp_ref[0, hh] = rope(head_norm(kall[:, sl] + krot, kng)).astype(BF16)
    vt = _dot_nt(wuvt_ref[...], ckvn)
    for hh in range(n_heads):
        vt_ref[0, hh] = vt[hh * dv_mla:(hh + 1) * dv_mla, :].astype(BF16)

    sg_ref[...] = _sigmoid(_dot(h, wg_ref[...])).astype(BF16)


def _stage_inproj(x, positions, p, *, tm):
    b, s, d = x.shape
    n = b * s
    nt = s // tm
    hm = p["n_mla_heads"]
    qk_w = p["qk_w"]
    body = functools.partial(
        _inproj_body, tm=tm, qk_w=qk_w, dk=p["dk"], n_heads=p["n_gdn_heads"],
        d_rot_lo=p["nope"], d_rot_half=p["rope"] // 2, d_qk=p["mla_qk"], dv_mla=p["mla_v"],
        conv_k=p["conv_k"])
    weights = [p["n1g"], p["wqkv"], p["convw"], p["seg"], p["wz"], p["wbg"], p["bgpar"], p["wcq"],
               p["cqg"], p["wuq"], p["qng"], p["wckv"], p["ckvg"], p["wuk"], p["wuvt"], p["wkrot"],
               p["kng"], p["ropepar"], p["wg"]]
    flat = lambda w: pl.BlockSpec((tm, w), lambda bi, si: (bi * nt + si, 0))
    in_specs = [
        pl.BlockSpec((1, tm, d), lambda bi, si: (bi, si, 0)),
        pl.BlockSpec((1, 1, 8, d), lambda bi, si: (bi, jnp.maximum(si * (tm // 8) - 1, 0), 0, 0)),
        pl.BlockSpec((1, tm, 1), lambda bi, si: (bi, si, 0)),
    ] + [_full(w.shape) for w in weights]
    v_w = p["v_w"]
    out_shape = [
        jax.ShapeDtypeStruct((n, qk_w), BF16), jax.ShapeDtypeStruct((n, qk_w), BF16),
        jax.ShapeDtypeStruct((n, v_w), BF16), jax.ShapeDtypeStruct((n, v_w), BF16),
        jax.ShapeDtypeStruct((n, LANES), F32), jax.ShapeDtypeStruct((n, 2 * d), BF16),
        jax.ShapeDtypeStruct((b, hm, s, LANES), BF16), jax.ShapeDtypeStruct((b, hm, s, LANES), BF16),
        jax.ShapeDtypeStruct((b, hm, p["mla_v"], s), BF16),
    ]
    out_specs = [
        flat(qk_w), flat(qk_w), flat(v_w), flat(v_w), flat(LANES), flat(2 * d),
        pl.BlockSpec((1, hm, tm, LANES), lambda bi, si: (bi, 0, si, 0)),
        pl.BlockSpec((1, hm, tm, LANES), lambda bi, si: (bi, 0, si, 0)),
        pl.BlockSpec((1, hm, p["mla_v"], tm), lambda bi, si: (bi, 0, 0, si)),
    ]
    return pl.pallas_call(
        body, grid=(b, nt), in_specs=in_specs, out_specs=out_specs, out_shape=out_shape,
        compiler_params=pltpu.CompilerParams(
            dimension_semantics=("arbitrary", "arbitrary"), vmem_limit_bytes=VMEM_LIMIT),
        name="inproj",
    )(x, x.reshape(b, s // 8, 8, d), positions.reshape(b, s, 1), *weights)


def _gdn_body(q_ref, k_ref, v_ref, bg_ref, sz_ref, gng_ref, o_ref, state_ref, *, n_chunks, n_heads, dk, dv):
    c = CHUNK

    @pl.when(pl.program_id(1) == 0)
    def _():
        state_ref[...] = jnp.zeros_like(state_ref)

    r_io = lax.broadcasted_iota(jnp.int32, (c, c), 0)
    c_io = lax.broadcasted_iota(jnp.int32, (c, c), 1)
    causal = r_io >= c_io
    strict = r_io > c_io
    tri_incl = causal.astype(F32)
    tri_upper = (r_io <= c_io).astype(F32)
    eye = (r_io == c_io).astype(F32)
    same16 = (r_io // 16) == (c_io // 16)
    same32 = (r_io // 32) == (c_io // 32)
    gng = gng_ref[...]

    def mm(a, b):
        return _dot(a.astype(BF16), b.astype(BF16))

    def chunk_step(ci, carry):
        rows = pl.ds(pl.multiple_of(ci * c, c), c)
        bg = bg_ref[rows, :]
        gc_cols = jnp.dot(tri_incl, bg, preferred_element_type=F32, precision=lax.Precision.HIGHEST)
        gc_rows = lax.dot_general(bg, tri_upper, TN_DIMS, preferred_element_type=F32,
                                  precision=lax.Precision.HIGHEST)
        for hh in range(n_heads):
            qh = q_ref[rows, hh * dk:(hh + 1) * dk].astype(F32)
            kh = k_ref[rows, hh * dk:(hh + 1) * dk].astype(F32)
            vh = v_ref[rows, hh * dv:(hh + 1) * dv].astype(F32)
            beta = bg[:, hh:hh + 1]
            gcc = gc_cols[:, n_heads + hh:n_heads + hh + 1]
            gcr = gc_rows[n_heads + hh:n_heads + hh + 1, :]
            glast = gcc[c - 1:c, :]
            dec = jnp.where(causal, jnp.exp(jnp.minimum(gcc - gcr, 0.0)), 0.0)
            kb = kh * beta
            eg = jnp.exp(gcc)
            akq = _dot_nt(jnp.concatenate([kb, qh], axis=0).astype(BF16), kh.astype(BF16))
            lower = jnp.where(strict, akq[:c] * dec, 0.0)
            attn = akq[c:] * dec
            nd = jnp.where(same16, -lower, 0.0)
            t = eye + nd
            npow = nd
            for _ in range(3):
                npow = mm(npow, npow)
                t = t + mm(t, npow)
            b1 = jnp.where(jnp.logical_and(same32, jnp.logical_not(same16)), lower, 0.0)
            t = t - mm(mm(t, b1), t)
            b2 = jnp.where(same32, 0.0, lower)
            t = t - mm(mm(t, b2), t)
            uw = mm(t, jnp.concatenate([vh * beta, kb * eg], axis=1))
            u = uw[:, :dv]
            w = uw[:, dv:]
            st = state_ref[hh]
            wqs = mm(jnp.concatenate([w, qh * eg], axis=0), st)
            v_new = u - wqs[:c]
            o = wqs[c:] + mm(attn, v_new)
            kdec = kh * jnp.exp(glast - gcc)
            state_ref[hh] = st * jnp.exp(glast) + _dot_tn(kdec.astype(BF16), v_new.astype(BF16))
            ms = jnp.mean(o * o, axis=-1, keepdims=True)
            on = o * lax.rsqrt(ms + NORM_EPS) * gng
            o_ref[rows, hh * dv:(hh + 1) * dv] = (on * sz_ref[rows, hh * dv:(hh + 1) * dv].astype(F32)).astype(BF16)
        return carry

    lax.fori_loop(0, n_chunks, chunk_step, 0)


def _stage_gdn(gq, gk, gv, bgc, sz, p, *, b, s, ct):
    n = b * s
    nt = s // ct
    hg, dk, dv = p["n_gdn_heads"], p["dk"], p["dv"]
    body = functools.partial(_gdn_body, n_chunks=ct // CHUNK, n_heads=hg, dk=dk, dv=dv)
    flat = lambda w: pl.BlockSpec((ct, w), lambda bi, si: (bi * nt + si, 0))
    return pl.pallas_call(
        body, grid=(b, nt),
        in_specs=[flat(hg * dk), flat(hg * dk), flat(hg * dv), flat(LANES), flat(hg * dv), _full(p["gng"].shape)],
        out_specs=flat(hg * dv),
        out_shape=jax.ShapeDtypeStruct((n, hg * dv), BF16),
        scratch_shapes=[pltpu.VMEM((hg, dk, dv), F32)],
        compiler_params=pltpu.CompilerParams(
            dimension_semantics=("arbitrary", "arbitrary"), vmem_limit_bytes=VMEM_LIMIT),
        name="gdn",
    )(gq, gk, gv, bgc, sz, p["gng"])


def _attn_body(itab_ref, jtab_ref, q_ref, k_ref, vt_ref, o_ref, acc_ref, m_ref, l_ref, *, n_heads, dv, bq, bk):
    pidx = pl.program_id(1)
    i = itab_ref[pidx]
    j = jtab_ref[pidx]

    @pl.when(j == 0)
    def _():
        acc_ref[...] = jnp.zeros_like(acc_ref)
        m_ref[...] = jnp.full_like(m_ref, -jnp.inf)
        l_ref[...] = jnp.zeros_like(l_ref)

    def step(masked):
        if masked:
            kc = (j * bk + lax.broadcasted_iota(jnp.int32, (bk, bq), 0)) // CHUNK
            qc = (i * bq + lax.broadcasted_iota(jnp.int32, (bk, bq), 1)) // CHUNK
            allowed = kc <= qc
        for hh in range(n_heads):
            st = _dot_nt(k_ref[0, hh], q_ref[0, hh])
            if masked:
                st = jnp.where(allowed, st, -jnp.inf)
            m_prev = m_ref[hh:hh + 1, :]
            m_new = jnp.maximum(m_prev, jnp.max(st, axis=0, keepdims=True))
            alpha = jnp.exp(m_prev - m_new)
            pt = jnp.exp(st - m_new)
            l_ref[hh:hh + 1, :] = alpha * l_ref[hh:hh + 1, :] + jnp.sum(pt, axis=0, keepdims=True)
            m_ref[hh:hh + 1, :] = m_new
            rows = slice(hh * dv, (hh + 1) * dv)
            acc_ref[rows, :] = acc_ref[rows, :] * alpha + _dot(vt_ref[0, hh], pt.astype(BF16))

    @pl.when(j < i)
    def _():
        step(False)

    @pl.when(j == i)
    def _():
        step(True)
        inv = 1.0 / l_ref[...]
        for hh in range(n_heads):
            rows = slice(hh * dv, (hh + 1) * dv)
            acc_ref[rows, :] = acc_ref[rows, :] * inv[hh:hh + 1, :]
        o_ref[0] = acc_ref[...].T.astype(BF16)


def _stage_attn(qp, kp, vt, *, blk):
    b, hm, s, _ = qp.shape
    dv = vt.shape[2]
    nb = s // blk
    pairs = [(i, j) for i in range(nb) for j in range(i + 1)]
    itab = jnp.asarray(np.array([pq[0] for pq in pairs], np.int32))
    jtab = jnp.asarray(np.array([pq[1] for pq in pairs], np.int32))
    body = functools.partial(_attn_body, n_heads=hm, dv=dv, bq=blk, bk=blk)
    grid_spec = pltpu.PrefetchScalarGridSpec(
        num_scalar_prefetch=2, grid=(b, len(pairs)),
        in_specs=[
            pl.BlockSpec((1, hm, blk, LANES), lambda bi, pi, it, jt: (bi, 0, it[pi], 0)),
            pl.BlockSpec((1, hm, blk, LANES), lambda bi, pi, it, jt: (bi, 0, jt[pi], 0)),
            pl.BlockSpec((1, hm, dv, blk), lambda bi, pi, it, jt: (bi, 0, 0, jt[pi])),
        ],
        out_specs=pl.BlockSpec((1, blk, hm * dv), lambda bi, pi, it, jt: (bi, it[pi], 0)),
        scratch_shapes=[pltpu.VMEM((hm * dv, blk), F32), pltpu.VMEM((hm, blk), F32), pltpu.VMEM((hm, blk), F32)],
    )
    return pl.pallas_call(
        body, grid_spec=grid_spec,
        out_shape=jax.ShapeDtypeStruct((b, s, hm * dv), BF16),
        compiler_params=pltpu.CompilerParams(
            dimension_semantics=("arbitrary", "arbitrary"), vmem_limit_bytes=VMEM_LIMIT),
        name="attn",
    )(itab, jtab, qp, kp, vt)


def _merge_body(x_ref, oa_ref, att_ref, sg_ref, wgo_ref, wmo_ref, wo_ref, o_ref, *, d):
    ya = _dot(oa_ref[...], wgo_ref[...])
    yb = _dot(att_ref[...], wmo_ref[...])
    merged = sg_ref[:, :d].astype(F32) * ya + sg_ref[:, d:].astype(F32) * yb
    o_ref[...] = x_ref[...] + _dot(merged.astype(BF16), wo_ref[...])


def _stage_merge(x2, oa, att, sg, p, *, tm):
    n, d = x2.shape
    row = lambda w: pl.BlockSpec((tm, w), lambda i: (i, 0))
    return pl.pallas_call(
        functools.partial(_merge_body, d=d), grid=(n // tm,),
        in_specs=[row(d), row(oa.shape[1]), row(att.shape[1]), row(2 * d),
                  _full(p["wgo"].shape), _full(p["wmo"].shape), _full(p["wo"].shape)],
        out_specs=row(d), out_shape=jax.ShapeDtypeStruct((n, d), F32),
        compiler_params=pltpu.CompilerParams(dimension_semantics=("arbitrary",), vmem_limit_bytes=VMEM_LIMIT),
        name="merge",
    )(x2, oa, att, sg, p["wgo"], p["wmo"], p["wo"])


def _erf_gelu(v):
    return 0.5 * v * (1.0 + lax.erf(v * (2.0 ** -0.5)))


def _peer_body(x_ref, n2g_ref, wpqt_ref, keys_ref, u_ref, vt_ref, o_ref,
               ht_ref, yt_ref, s2_ref, e2_ref, c1_ref, e1_ref, top_ref, cand_ref, p_ref,
               *, tt, n_heads, n_keys, topk, ec, cand_pairs):
    e = pl.program_id(1)
    n_chunks = pl.num_programs(1)
    neg_inf = -jnp.inf

    @pl.when(e == 0)
    def _():
        h2 = _rms(x_ref[...], n2g_ref[...])
        ht = h2.T.astype(BF16)
        ht_ref[...] = ht
        yt_ref[...] = jnp.zeros_like(yt_ref)
        qt = _dot(wpqt_ref[...], ht).astype(BF16)
        for hh in range(n_heads):
            tops = []
            for half in range(2):
                r0 = (hh * 2 + half) * n_keys
                sc = _dot(keys_ref[r0:r0 + n_keys, :], qt[r0:r0 + n_keys, :])
                if half == 0:
                    c1_ref[hh] = sc
                else:
                    s2_ref[hh] = sc
                work = sc
                for r in range(topk):
                    mx = jnp.max(work, axis=0, keepdims=True)
                    top_ref[half * topk + r:half * topk + r + 1, :] = mx
                    work = jnp.where(work == mx, neg_inf, work)
            for ci, (r1, r2) in enumerate(cand_pairs):
                cand_ref[ci:ci + 1, :] = top_ref[r1:r1 + 1, :] + top_ref[topk + r2:topk + r2 + 1, :]
            n_cand = len(cand_pairs)
            pad = cand_ref.shape[0] - n_cand
            if pad:
                cand_ref[n_cand:, :] = jnp.full((pad, tt), neg_inf, F32)
            work = cand_ref[...]
            smax = top_ref[0:1, :] + top_ref[topk:topk + 1, :]
            zsum = jnp.zeros((1, tt), F32)
            kth = smax
            for r in range(topk + 1):
                mx = jnp.max(work, axis=0, keepdims=True)
                if r < topk:
                    zsum = zsum + jnp.exp(mx - smax)
                    kth = mx
                else:
                    thr = 0.5 * (kth + mx)
                work = jnp.where(work == mx, neg_inf, work)
            s1 = c1_ref[hh]
            e1_ref[hh] = jnp.exp(s1 - top_ref[0:1, :]) / zsum
            c1_ref[hh] = thr - s1
            e2_ref[hh] = jnp.exp(s2_ref[hh] - top_ref[topk:topk + 1, :])

    ht = ht_ref[...]
    for il in range(ec // n_keys):
        act = _dot(u_ref[il * n_keys:(il + 1) * n_keys, :], ht)
        ig = e * (ec // n_keys) + il
        wsum = jnp.zeros((n_keys, tt), F32)
        for hh in range(n_heads):
            c1 = c1_ref[hh, pl.ds(ig, 1), :]
            e1 = e1_ref[hh, pl.ds(ig, 1), :]
            wsum = wsum + jnp.where(s2_ref[hh] >= c1, e1 * e2_ref[hh], 0.0)
        p_ref[il * n_keys:(il + 1) * n_keys, :] = (_erf_gelu(act) * wsum).astype(BF16)
    yt_ref[...] += _dot(vt_ref[...], p_ref[...])

    @pl.when(e == n_chunks - 1)
    def _():
        o_ref[...] = x_ref[...] + yt_ref[...].T


def _stage_peer(x1, p, *, tt, ec):
    n, d = x1.shape
    hp, nk, topk = p["n_peer_heads"], p["n_keys"], PEER_TOPK
    ne = p["u"].shape[0]
    cand_pairs = tuple((a, c) for a in range(topk) for c in range(topk) if (a + 1) * (c + 1) <= topk)
    n_cand_rows = -(-len(cand_pairs) // 8) * 8
    body = functools.partial(_peer_body, tt=tt, n_heads=hp, n_keys=nk, topk=topk, ec=ec, cand_pairs=cand_pairs)
    return pl.pallas_call(
        body, grid=(n // tt, ne // ec),
        in_specs=[
            pl.BlockSpec((tt, d), lambda ti, ei: (ti, 0)),
            _full(p["n2g"].shape), _full(p["wpqt"].shape), _full(p["keys"].shape),
            pl.BlockSpec((ec, d), lambda ti, ei: (ei, 0)),
            pl.BlockSpec((d, ec), lambda ti, ei: (0, ei)),
        ],
        out_specs=pl.BlockSpec((tt, d), lambda ti, ei: (ti, 0)),
        out_shape=jax.ShapeDtypeStruct((n, d), F32),
        scratch_shapes=[
            pltpu.VMEM((d, tt), BF16),
            pltpu.VMEM((d, tt), F32),
            pltpu.VMEM((hp, nk, tt), F32),
            pltpu.VMEM((hp, nk, tt), F32),
            pltpu.VMEM((hp, nk, tt), F32),
            pltpu.VMEM((hp, nk, tt), F32),
            pltpu.VMEM((2 * topk, tt), F32),
            pltpu.VMEM((n_cand_rows, tt), F32),
            pltpu.VMEM((ec, tt), BF16),
        ],
        compiler_params=pltpu.CompilerParams(
            dimension_semantics=("arbitrary", "arbitrary"), vmem_limit_bytes=VMEM_LIMIT),
        name="peer",
    )(x1, p["n2g"], p["wpqt"], p["keys"], p["u"], p["vt"])


def _prepare(norm1_g, w_in, conv_w, a_log, dt_bias, gdn_norm_g, w_gdn_out, cq_norm_g, w_uq, ckv_norm_g,
             w_ukv, q_norm_g, k_norm_g, w_mla_out, w_o, norm2_g, w_pq, sub_keys, u_tab, v_tab):
    d = w_in.shape[1]
    hg = a_log.shape[-1]
    dv = gdn_norm_g.shape[-1]
    v_w = w_gdn_out.shape[1]
    qkv_w = conv_w.shape[-1]
    qk_w = (qkv_w - v_w) // 2
    dk = qk_w // hg
    q_lora = cq_norm_g.shape[-1]
    kv_lora = ckv_norm_g.shape[-1]
    mla_qk = q_norm_g.shape[-1]
    hm = w_uq.shape[-1] // mla_qk
    mla_v = w_mla_out.shape[1] // hm
    nope = w_ukv.shape[-1] // hm - mla_v
    rope = mla_qk - nope
    assert mla_qk <= LANES and 2 * hg <= LANES and dk == dv

    win = w_in[0]
    o = 0
    cols = {}
    for name, width in (("qkv", qkv_w), ("z", v_w), ("b", hg), ("a", hg), ("cq", q_lora), ("ckv", kv_lora),
                        ("krot", rope), ("gates", 2 * d)):
        cols[name] = win[:, o:o + width]
        o += width
    assert o == win.shape[1]

    p = dict(n_gdn_heads=hg, dk=dk, dv=dv, qk_w=qk_w, v_w=v_w, n_mla_heads=hm, mla_qk=mla_qk, mla_v=mla_v,
             nope=nope, rope=rope, conv_k=conv_w.shape[1])
    p["n1g"] = norm1_g[0][None, :]
    p["wqkv"] = cols["qkv"].astype(BF16)
    p["convw"] = conv_w[0]
    head_of = np.arange(qk_w) // dk
    p["seg"] = jnp.asarray((head_of[:, None] == head_of[None, :]).astype(np.float32)).astype(BF16)
    p["wz"] = cols["z"].astype(BF16)
    wbg = jnp.zeros((d, LANES), F32).at[:, :hg].set(cols["b"]).at[:, hg:2 * hg].set(cols["a"])
    p["wbg"] = wbg.astype(BF16)
    nexp_a = jnp.zeros((LANES,), F32).at[hg:2 * hg].set(-jnp.exp(a_log[0]))
    dtb = jnp.zeros((LANES,), F32).at[hg:2 * hg].set(dt_bias[0])
    p["bgpar"] = jnp.stack([nexp_a, dtb])
    p["wcq"] = cols["cq"].astype(BF16)
    p["cqg"] = cq_norm_g[0][None, :]
    wuq = w_uq[0].reshape(q_lora, hm, mla_qk)
    p["wuq"] = jnp.pad(wuq, ((0, 0), (0, 0), (0, LANES - mla_qk))).reshape(q_lora, hm * LANES).astype(BF16)
    p["qng"] = jnp.pad(q_norm_g[0] * (mla_qk ** -0.5), (0, LANES - mla_qk))[None, :]
    p["wckv"] = cols["ckv"].astype(BF16)
    p["ckvg"] = ckv_norm_g[0][None, :]
    wukv = w_ukv[0].reshape(kv_lora, hm, nope + mla_v)
    p["wuk"] = jnp.pad(wukv[:, :, :nope], ((0, 0), (0, 0), (0, LANES - nope))).reshape(kv_lora, hm * LANES).astype(BF16)
    p["wuvt"] = wukv[:, :, nope:].reshape(kv_lora, hm * mla_v).T.astype(BF16)
    p["wkrot"] = jnp.zeros((d, LANES), F32).at[:, nope:nope + rope].set(cols["krot"]).astype(BF16)
    p["kng"] = jnp.pad(k_norm_g[0], (0, LANES - mla_qk))[None, :]
    half = rope // 2
    inv = np.power(ROPE_THETA, -np.arange(half, dtype=np.float32) / half).astype(np.float32)
    invf = np.zeros((LANES,), np.float32)
    invf[nope:nope + half] = inv
    invf[nope + half:nope + rope] = inv
    sign = np.zeros((LANES,), np.float32)
    sign[nope:nope + half] = -1.0
    sign[nope + half:nope + rope] = 1.0
    p["ropepar"] = jnp.asarray(np.stack([invf, sign]))
    p["wg"] = cols["gates"].astype(BF16)

    p["gng"] = gdn_norm_g[0][None, :]
    p["wgo"] = w_gdn_out[0].astype(BF16)
    p["wmo"] = w_mla_out[0].astype(BF16)
    p["wo"] = w_o[0].astype(BF16)

    hp, _, nk, dkey = sub_keys.shape[1:]
    p["n_peer_heads"] = hp
    p["n_keys"] = nk
    assert nk == LANES and dkey == LANES
    p["n2g"] = norm2_g[0][None, :]
    p["wpqt"] = w_pq[0].T.astype(BF16)
    p["keys"] = sub_keys[0].reshape(hp * 2 * nk, dkey).astype(BF16)
    p["u"] = u_tab[0].astype(BF16)
    p["vt"] = v_tab[0].T.astype(BF16)
    return p


def kernel(x, positions, norm1_g, w_in, conv_w, a_log, dt_bias, gdn_norm_g, w_gdn_out, cq_norm_g, w_uq,
           ckv_norm_g, w_ukv, q_norm_g, k_norm_g, w_mla_out, w_o, norm2_g, w_pq, sub_keys, u_tab, v_tab):
    assert w_in.shape[0] == 1, "single-layer block"
    b, s, d = x.shape
    p = _prepare(norm1_g, w_in, conv_w, a_log, dt_bias, gdn_norm_g, w_gdn_out, cq_norm_g, w_uq, ckv_norm_g,
                 w_ukv, q_norm_g, k_norm_g, w_mla_out, w_o, norm2_g, w_pq, sub_keys, u_tab, v_tab)
    tm = min(256, s)
    gq, gk, gv, sz, bgc, sg, qp, kp, vt = _stage_inproj(x, positions, p, tm=tm)
    oa = _stage_gdn(gq, gk, gv, bgc, sz, p, b=b, s=s, ct=min(256, s))
    att = _stage_attn(qp, kp, vt, blk=min(512, s))
    x1 = _stage_merge(x.reshape(b * s, d), oa, att.reshape(b * s, -1), sg, p, tm=min(512, s))
    out = _stage_peer(x1, p, tt=min(512, b * s), ec=1024)
    return out.reshape(b, s, d)
```

```python
import functools

import jax
import jax.numpy as jnp
import numpy as np
from jax import lax
from jax.experimental import pallas as pl
from jax.experimental.pallas import tpu as pltpu

F32 = jnp.float32
BF16 = jnp.bfloat16

CHUNK = 64
NORM_EPS = 1e-6
ROPE_THETA = 10000.0
PEER_TOPK = 16
LANES = 128
VMEM_LIMIT = 56 * 1024 * 1024

NT_DIMS = (((1,), (1,)), ((), ()))
TN_DIMS = (((0,), (0,)), ((), ()))


def _dot(a, b):
    return jnp.dot(a, b, preferred_element_type=F32)


def _dot_nt(a, b):
    return lax.dot_general(a, b, NT_DIMS, preferred_element_type=F32)


def _dot_tn(a, b):
    return lax.dot_general(a, b, TN_DIMS, preferred_element_type=F32)


def _rms(v, g):
    ms = jnp.mean(v * v, axis=-1, keepdims=True)
    return v * lax.rsqrt(ms + NORM_EPS) * g


def _sigmoid(v):
    return 1.0 / (1.0 + jnp.exp(-v))


def _softplus(v):
    return jnp.maximum(v, 0.0) + jnp.log(1.0 + jnp.exp(-jnp.abs(v)))


def _full(shape):
    n = len(shape)
    return pl.BlockSpec(shape, lambda *_: (0,) * n)


def _inproj_body(x_ref, xprev_ref, pos_ref, n1g_ref, wqkv_ref, convw_ref, seg_ref, wz_ref,
                 wbg_ref, bgpar_ref, wcq_ref, cqg_ref, wuq_ref, qng_ref, wckv_ref, ckvg_ref,
                 wuk_ref, wuvt_ref, wkrot_ref, kng_ref, ropepar_ref, wg_ref,
                 gq_ref, gk_ref, gv_ref, sz_ref, bgc_ref, sg_ref, qp_ref, kp_ref, vt_ref,
                 *, tm, qk_w, dk, n_heads, d_rot_lo, d_rot_half, d_qk, dv_mla, conv_k):
    s = pl.program_id(1)
    n1g = n1g_ref[...]
    h = _rms(x_ref[0], n1g).astype(BF16)
    hp = _rms(xprev_ref[0, 0], n1g).astype(BF16)

    pre = _dot(h, wqkv_ref[...])
    pre_prev = _dot(hp, wqkv_ref[...])
    pre_prev = jnp.where(s == 0, 0.0, pre_prev)
    ext = jnp.concatenate([pre_prev, pre], axis=0)
    cw = convw_ref[...]
    conv = cw[0:1, :] * ext[8 - conv_k + 1:8 - conv_k + 1 + tm, :]
    for k in range(1, conv_k):
        off = 8 - conv_k + 1 + k
        conv = conv + cw[k:k + 1, :] * ext[off:off + tm, :]
    act = conv * _sigmoid(conv)
    seg = seg_ref[...]

    def l2n(t):
        ss = _dot((t * t).astype(BF16), seg)
        return t * lax.rsqrt(ss + NORM_EPS)

    gq_ref[...] = (l2n(act[:, :qk_w]) * (dk ** -0.5)).astype(BF16)
    gk_ref[...] = l2n(act[:, qk_w:2 * qk_w]).astype(BF16)
    gv_ref[...] = act[:, 2 * qk_w:].astype(BF16)

    z = _dot(h, wz_ref[...])
    sz_ref[...] = (z * _sigmoid(z)).astype(BF16)

    raw = _dot(h, wbg_ref[...])
    lane = lax.broadcasted_iota(jnp.int32, raw.shape, 1)
    g = bgpar_ref[0:1, :] * _softplus(raw + bgpar_ref[1:2, :])
    bgc_ref[...] = jnp.where(lane < n_heads, _sigmoid(raw), g)

    pos = pos_ref[0].astype(F32)
    ang = pos * ropepar_ref[0:1, :]
    cosv = jnp.cos(ang)
    sinv = jnp.sin(ang) * ropepar_ref[1:2, :]
    lane128 = lax.broadcasted_iota(jnp.int32, ang.shape, 1)
    first_half = lane128 < d_rot_lo + d_rot_half

    def rope(y):
        partner = jnp.where(first_half, pltpu.roll(y, LANES - d_rot_half, 1), pltpu.roll(y, d_rot_half, 1))
        return y * cosv + partner * sinv

    def head_norm(slab, gain):
        ms = jnp.sum(slab * slab, axis=-1, keepdims=True) * (1.0 / d_qk)
        return slab * lax.rsqrt(ms + NORM_EPS) * gain

    cqn = _rms(_dot(h, wcq_ref[...]), cqg_ref[...]).astype(BF16)
    qall = _dot(cqn, wuq_ref[...])
    ckvn = _rms(_dot(h, wckv_ref[...]), ckvg_ref[...]).astype(BF16)
    kall = _dot(ckvn, wuk_ref[...])
    krot = _dot(h, wkrot_ref[...])
    qng = qng_ref[...]
    kng = kng_ref[...]
    for hh in range(n_heads):
        sl = slice(hh * LANES, (hh + 1) * LANES)
        qp_ref[0, hh] = rope(head_norm(qall[:, sl], qng)).astype(BF16)
        kp_ref[0, hh] = rope(head_norm(kall[:, sl] + krot, kng)).astype(BF16)
    vt = _dot_nt(wuvt_ref[...], ckvn)
    for hh in range(n_heads):
        vt_ref[0, hh] = vt[hh * dv_mla:(hh + 1) * dv_mla, :].astype(BF16)

    sg_ref[...] = _sigmoid(_dot(h, wg_ref[...])).astype(BF16)


def _stage_inproj(x, positions, p, *, tm):
    b, s, d = x.shape
    n = b * s
    nt = s // tm
    hm = p["n_mla_heads"]
    qk_w = p["qk_w"]
    body = functools.partial(
        _inproj_body, tm=tm, qk_w=qk_w, dk=p["dk"], n_heads=p["n_gdn_heads"],
        d_rot_lo=p["nope"], d_rot_half=p["rope"] // 2, d_qk=p["mla_qk"], dv_mla=p["mla_v"],
        conv_k=p["conv_k"])
    weights = [p["n1g"], p["wqkv"], p["convw"], p["seg"], p["wz"], p["wbg"], p["bgpar"], p["wcq"],
               p["cqg"], p["wuq"], p["qng"], p["wckv"], p["ckvg"], p["wuk"], p["wuvt"], p["wkrot"],
               p["kng"], p["ropepar"], p["wg"]]
    flat = lambda w: pl.BlockSpec((tm, w), lambda bi, si: (bi * nt + si, 0))
    in_specs = [
        pl.BlockSpec((1, tm, d), lambda bi, si: (bi, si, 0)),
        pl.BlockSpec((1, 1, 8, d), lambda bi, si: (bi, jnp.maximum(si * (tm // 8) - 1, 0), 0, 0)),
        pl.BlockSpec((1, tm, 1), lambda bi, si: (bi, si, 0)),
    ] + [_full(w.shape) for w in weights]
    v_w = p["v_w"]
    out_shape = [
        jax.ShapeDtypeStruct((n, qk_w), BF16), jax.ShapeDtypeStruct((n, qk_w), BF16),
        jax.ShapeDtypeStruct((n, v_w), BF16), jax.ShapeDtypeStruct((n, v_w), BF16),
        jax.ShapeDtypeStruct((n, LANES), F32), jax.ShapeDtypeStruct((n, 2 * d), BF16),
        jax.ShapeDtypeStruct((b, hm, s, LANES), BF16), jax.ShapeDtypeStruct((b, hm, s, LANES), BF16),
        jax.ShapeDtypeStruct((b, hm, p["mla_v"], s), BF16),
    ]
    out_specs = [
        flat(qk_w), flat(qk_w), flat(v_w), flat(v_w), flat(LANES), flat(2 * d),
        pl.BlockSpec((1, hm, tm, LANES), lambda bi, si: (bi, 0, si, 0)),
        pl.BlockSpec((1, hm, tm, LANES), lambda bi, si: (bi, 0, si, 0)),
        pl.BlockSpec((1, hm, p["mla_v"], tm), lambda bi, si: (bi, 0, 0, si)),
    ]
    return pl.pallas_call(
        body, grid=(b, nt), in_specs=in_specs, out_specs=out_specs, out_shape=out_shape,
        compiler_params=pltpu.CompilerParams(
            dimension_semantics=("arbitrary", "arbitrary"), vmem_limit_bytes=VMEM_LIMIT),
        name="inproj",
    )(x, x.reshape(b, s // 8, 8, d), positions.reshape(b, s, 1), *weights)


def _gdn_body(q_ref, k_ref, v_ref, bg_ref, sz_ref, gng_ref, o_ref, state_ref, *, n_chunks, n_heads, dk, dv):
    c = CHUNK

    @pl.when(pl.program_id(1) == 0)
    def _():
        state_ref[...] = jnp.zeros_like(state_ref)

    r_io = lax.broadcasted_iota(jnp.int32, (c, c), 0)
    c_io = lax.broadcasted_iota(jnp.int32, (c, c), 1)
    causal = r_io >= c_io
    strict = r_io > c_io
    tri_incl = causal.astype(F32)
    tri_upper = (r_io <= c_io).astype(F32)
    eye = (r_io == c_io).astype(F32)
    same16 = (r_io // 16) == (c_io // 16)
    same32 = (r_io // 32) == (c_io // 32)
    gng = gng_ref[...]

    def mm(a, b):
        return _dot(a.astype(BF16), b.astype(BF16))

    def chunk_step(ci, carry):
        rows = pl.ds(pl.multiple_of(ci * c, c), c)
        bg = bg_ref[rows, :]
        gc_cols = jnp.dot(tri_incl, bg, preferred_element_type=F32, precision=lax.Precision.HIGHEST)
        gc_rows = lax.dot_general(bg, tri_upper, TN_DIMS, preferred_element_type=F32,
                                  precision=lax.Precision.HIGHEST)
        q_all = q_ref[rows, :].astype(F32)
        k_all = k_ref[rows, :].astype(F32)
        v_all = v_ref[rows, :].astype(F32)
        sz_all = sz_ref[rows, :].astype(F32)
        states = [state_ref[hh] for hh in range(n_heads)]
        heads = range(n_heads)
        qs = [q_all[:, hh * dk:(hh + 1) * dk] for hh in heads]
        ks = [k_all[:, hh * dk:(hh + 1) * dk] for hh in heads]
        vs = [v_all[:, hh * dv:(hh + 1) * dv] for hh in heads]
        betas = [bg[:, hh:hh + 1] for hh in heads]
        gccs = [gc_cols[:, n_heads + hh:n_heads + hh + 1] for hh in heads]
        gcrs = [gc_rows[n_heads + hh:n_heads + hh + 1, :] for hh in heads]
        glasts = [g[c - 1:c, :] for g in gccs]
        decs = [jnp.where(causal, jnp.exp(jnp.minimum(gccs[hh] - gcrs[hh], 0.0)), 0.0) for hh in heads]
        kbs = [ks[hh] * betas[hh] for hh in heads]
        egs = [jnp.exp(g) for g in gccs]
        akqs = [_dot_nt(jnp.concatenate([kbs[hh], qs[hh]], axis=0).astype(BF16), ks[hh].astype(BF16))
                for hh in heads]
        lowers = [jnp.where(strict, akqs[hh][:c] * decs[hh], 0.0) for hh in heads]
        attns = [akqs[hh][c:] * decs[hh] for hh in heads]
        npows = [jnp.where(same16, -lo, 0.0) for lo in lowers]
        ts = [eye + nd for nd in npows]
        for _ in range(3):
            npows = [mm(a, a) for a in npows]
            ts = [ts[hh] + mm(ts[hh], npows[hh]) for hh in heads]
        off32 = jnp.logical_and(same32, jnp.logical_not(same16))
        tb = [mm(ts[hh], jnp.where(off32, lowers[hh], 0.0)) for hh in heads]
        ts = [ts[hh] - mm(tb[hh], ts[hh]) for hh in heads]
        tb = [mm(ts[hh], jnp.where(same32, 0.0, lowers[hh])) for hh in heads]
        ts = [ts[hh] - mm(tb[hh], ts[hh]) for hh in heads]
        uws = [mm(ts[hh], jnp.concatenate([vs[hh] * betas[hh], kbs[hh] * egs[hh]], axis=1))
               for hh in heads]
        wqss = [mm(jnp.concatenate([uws[hh][:, dv:], qs[hh] * egs[hh]], axis=0), states[hh])
                for hh in heads]
        v_news = [uws[hh][:, :dv] - wqss[hh][:c] for hh in heads]
        os_ = [wqss[hh][c:] + mm(attns[hh], v_news[hh]) for hh in heads]
        kdecs = [ks[hh] * jnp.exp(glasts[hh] - gccs[hh]) for hh in heads]
        new_states = [states[hh] * jnp.exp(glasts[hh]) + _dot_tn(kdecs[hh].astype(BF16), v_news[hh].astype(BF16))
                      for hh in heads]
        outs = [o * lax.rsqrt(jnp.mean(o * o, axis=-1, keepdims=True) + NORM_EPS) * gng for o in os_]
        o_ref[rows, :] = (jnp.concatenate(outs, axis=1) * sz_all).astype(BF16)
        for hh in range(n_heads):
            state_ref[hh] = new_states[hh]
        return carry

    lax.fori_loop(0, n_chunks, chunk_step, 0)


def _stage_gdn(gq, gk, gv, bgc, sz, p, *, b, s, ct):
    n = b * s
    nt = s // ct
    hg, dk, dv = p["n_gdn_heads"], p["dk"], p["dv"]
    body = functools.partial(_gdn_body, n_chunks=ct // CHUNK, n_heads=hg, dk=dk, dv=dv)
    flat = lambda w: pl.BlockSpec((ct, w), lambda bi, si: (bi * nt + si, 0))
    return pl.pallas_call(
        body, grid=(b, nt),
        in_specs=[flat(hg * dk), flat(hg * dk), flat(hg * dv), flat(LANES), flat(hg * dv), _full(p["gng"].shape)],
        out_specs=flat(hg * dv),
        out_shape=jax.ShapeDtypeStruct((n, hg * dv), BF16),
        scratch_shapes=[pltpu.VMEM((hg, dk, dv), F32)],
        compiler_params=pltpu.CompilerParams(
            dimension_semantics=("arbitrary", "arbitrary"), vmem_limit_bytes=VMEM_LIMIT),
        name="gdn",
    )(gq, gk, gv, bgc, sz, p["gng"])


def _attn_body(itab_ref, jtab_ref, q_ref, k_ref, vt_ref, o_ref, acc_ref, m_ref, l_ref, *, n_heads, dv, bq, bk):
    pidx = pl.program_id(1)
    i = itab_ref[pidx]
    j = jtab_ref[pidx]

    @pl.when(j == 0)
    def _():
        acc_ref[...] = jnp.zeros_like(acc_ref)
        m_ref[...] = jnp.full_like(m_ref, -jnp.inf)
        l_ref[...] = jnp.zeros_like(l_ref)

    def step(masked):
        if masked:
            kc = (j * bk + lax.broadcasted_iota(jnp.int32, (bk, bq), 0)) // CHUNK
            qc = (i * bq + lax.broadcasted_iota(jnp.int32, (bk, bq), 1)) // CHUNK
            allowed = kc <= qc
        def scores(hh):
            return _dot_nt(k_ref[0, hh], q_ref[0, hh])

        st_next = scores(0)
        for hh in range(n_heads):
            st = st_next
            if hh + 1 < n_heads:
                st_next = scores(hh + 1)
            if masked:
                st = jnp.where(allowed, st, -jnp.inf)
            m_prev = m_ref[hh:hh + 1, :]
            m_new = jnp.maximum(m_prev, jnp.max(st, axis=0, keepdims=True))
            alpha = jnp.exp2(m_prev - m_new)
            pt = jnp.exp2(st - m_new)
            l_ref[hh:hh + 1, :] = alpha * l_ref[hh:hh + 1, :] + jnp.sum(pt, axis=0, keepdims=True)
            m_ref[hh:hh + 1, :] = m_new
            rows = slice(hh * dv, (hh + 1) * dv)
            acc_ref[rows, :] = acc_ref[rows, :] * alpha + _dot(vt_ref[0, hh], pt.astype(BF16))

    @pl.when(j < i)
    def _():
        step(False)

    @pl.when(j == i)
    def _():
        step(True)
        inv = 1.0 / l_ref[...]
        for hh in range(n_heads):
            rows = slice(hh * dv, (hh + 1) * dv)
            acc_ref[rows, :] = acc_ref[rows, :] * inv[hh:hh + 1, :]
        o_ref[0] = acc_ref[...].T.astype(BF16)


def _stage_attn(qp, kp, vt, *, blk):
    b, hm, s, _ = qp.shape
    dv = vt.shape[2]
    nb = s // blk
    pairs = [(i, j) for i in range(nb) for j in range(i + 1)]
    itab = jnp.asarray(np.array([pq[0] for pq in pairs], np.int32))
    jtab = jnp.asarray(np.array([pq[1] for pq in pairs], np.int32))
    body = functools.partial(_attn_body, n_heads=hm, dv=dv, bq=blk, bk=blk)
    grid_spec = pltpu.PrefetchScalarGridSpec(
        num_scalar_prefetch=2, grid=(b, len(pairs)),
        in_specs=[
            pl.BlockSpec((1, hm, blk, LANES), lambda bi, pi, it, jt: (bi, 0, it[pi], 0)),
            pl.BlockSpec((1, hm, blk, LANES), lambda bi, pi, it, jt: (bi, 0, jt[pi], 0)),
            pl.BlockSpec((1, hm, dv, blk), lambda bi, pi, it, jt: (bi, 0, 0, jt[pi])),
        ],
        out_specs=pl.BlockSpec((1, blk, hm * dv), lambda bi, pi, it, jt: (bi, it[pi], 0)),
        scratch_shapes=[pltpu.VMEM((hm * dv, blk), F32), pltpu.VMEM((hm, blk), F32), pltpu.VMEM((hm, blk), F32)],
    )
    return pl.pallas_call(
        body, grid_spec=grid_spec,
        out_shape=jax.ShapeDtypeStruct((b, s, hm * dv), BF16),
        compiler_params=pltpu.CompilerParams(
            dimension_semantics=("arbitrary", "arbitrary"), vmem_limit_bytes=VMEM_LIMIT),
        name="attn",
    )(itab, jtab, qp, kp, vt)


def _merge_body(x_ref, oa_ref, att_ref, sg_ref, wgo_ref, wmo_ref, wo_ref, o_ref, *, d):
    ya = _dot(oa_ref[...], wgo_ref[...])
    yb = _dot(att_ref[...], wmo_ref[...])
    merged = sg_ref[:, :d].astype(F32) * ya + sg_ref[:, d:].astype(F32) * yb
    o_ref[...] = x_ref[...] + _dot(merged.astype(BF16), wo_ref[...])


def _stage_merge(x2, oa, att, sg, p, *, tm):
    n, d = x2.shape
    row = lambda w: pl.BlockSpec((tm, w), lambda i: (i, 0))
    return pl.pallas_call(
        functools.partial(_merge_body, d=d), grid=(n // tm,),
        in_specs=[row(d), row(oa.shape[1]), row(att.shape[1]), row(2 * d),
                  _full(p["wgo"].shape), _full(p["wmo"].shape), _full(p["wo"].shape)],
        out_specs=row(d), out_shape=jax.ShapeDtypeStruct((n, d), F32),
        compiler_params=pltpu.CompilerParams(dimension_semantics=("arbitrary",), vmem_limit_bytes=VMEM_LIMIT),
        name="merge",
    )(x2, oa, att, sg, p["wgo"], p["wmo"], p["wo"])


def _erf_gelu(v):
    return 0.5 * v * (1.0 + lax.erf(v * (2.0 ** -0.5)))


def _peer_body(x_ref, n2g_ref, wpqt_ref, keys_ref, u_ref, vt_ref, o_ref,
               ht_ref, yt_ref, e2_ref, e1_ref, thr_ref, top_ref, cand_ref,
               *, tt, n_heads, n_keys, topk, ec, cand_pairs):
    e = pl.program_id(1)
    n_chunks = pl.num_programs(1)
    neg_inf = -jnp.inf

    @pl.when(e == 0)
    def _():
        h2 = _rms(x_ref[...], n2g_ref[...])
        ht = h2.T.astype(BF16)
        ht_ref[...] = ht
        yt_ref[...] = jnp.zeros_like(yt_ref)
        qt = _dot(wpqt_ref[...], ht).astype(BF16)
        for hh in range(n_heads):
            tops = []
            for half in range(2):
                r0 = (hh * 2 + half) * n_keys
                sc = _dot(keys_ref[r0:r0 + n_keys, :], qt[r0:r0 + n_keys, :])
                if half == 0:
                    e1_ref[hh] = sc
                else:
                    e2_ref[hh] = sc
                work = sc
                for r in range(topk):
                    mx = jnp.max(work, axis=0, keepdims=True)
                    top_ref[half * topk + r:half * topk + r + 1, :] = mx
                    work = jnp.where(work == mx, neg_inf, work)
            for ci, (r1, r2) in enumerate(cand_pairs):
                cand_ref[ci:ci + 1, :] = top_ref[r1:r1 + 1, :] + top_ref[topk + r2:topk + r2 + 1, :]
            n_cand = len(cand_pairs)
            pad = cand_ref.shape[0] - n_cand
            if pad:
                cand_ref[n_cand:, :] = jnp.full((pad, tt), neg_inf, F32)
            work = cand_ref[...]
            smax = top_ref[0:1, :] + top_ref[topk:topk + 1, :]
            zsum = jnp.zeros((1, tt), F32)
            kth = smax
            for r in range(topk + 1):
                mx = jnp.max(work, axis=0, keepdims=True)
                if r < topk:
                    zsum = zsum + jnp.exp(mx - smax)
                    kth = mx
                else:
                    thr = 0.5 * (kth + mx)
                work = jnp.where(work == mx, neg_inf, work)
            inv_z = 1.0 / zsum
            e1_ref[hh] = jnp.exp(e1_ref[hh] - top_ref[0:1, :]) * inv_z
            e2_ref[hh] = jnp.exp(e2_ref[hh] - top_ref[topk:topk + 1, :])
            thr_ref[hh:hh + 1, :] = jnp.exp(thr - smax) * inv_z

    n_il = ec // n_keys
    i0 = pl.multiple_of(e * n_il, n_il)
    e1_blk = [e1_ref[hh, pl.ds(i0, n_il), :] for hh in range(n_heads)]
    ht = ht_ref[...]

    def activation(il):
        return _dot(u_ref[il * n_keys:(il + 1) * n_keys, :], ht)

    act_next = activation(0)
    pws = []
    for il in range(n_il):
        act = act_next
        if il + 1 < n_il:
            act_next = activation(il + 1)
        if il % 2 == 0 and il >= 2:
            rows = slice((il - 2) * n_keys, il * n_keys)
            yt_ref[...] += _dot(vt_ref[:, rows], jnp.concatenate(pws[il - 2:il], axis=0))
        wsum = jnp.zeros((n_keys, tt), F32)
        for hh in range(n_heads):
            gate = e1_blk[hh][il:il + 1, :] * e2_ref[hh]
            wsum = wsum + jnp.where(gate >= thr_ref[hh:hh + 1, :], gate, 0.0)
        pws.append((_erf_gelu(act) * wsum).astype(BF16))
    rows = slice((n_il - 2) * n_keys, n_il * n_keys)
    yt_ref[...] += _dot(vt_ref[:, rows], jnp.concatenate(pws[n_il - 2:], axis=0))

    @pl.when(e == n_chunks - 1)
    def _():
        o_ref[...] = x_ref[...] + yt_ref[...].T


def _stage_peer(x1, p, *, tt, ec):
    n, d = x1.shape
    hp, nk, topk = p["n_peer_heads"], p["n_keys"], PEER_TOPK
    nc = p["u"].shape[0] // ec
    cand_pairs = tuple((a, c) for a in range(topk) for c in range(topk) if (a + 1) * (c + 1) <= topk)
    n_cand_rows = -(-len(cand_pairs) // 8) * 8
    body = functools.partial(_peer_body, tt=tt, n_heads=hp, n_keys=nk, topk=topk, ec=ec, cand_pairs=cand_pairs)
    return pl.pallas_call(
        body, grid=(n // tt, nc),
        in_specs=[
            pl.BlockSpec((tt, d), lambda ti, ei: (ti, 0)),
            _full(p["n2g"].shape), _full(p["wpqt"].shape), _full(p["keys"].shape),
            pl.BlockSpec((ec, d), lambda ti, ei: (ei, 0)),
            pl.BlockSpec((d, ec), lambda ti, ei: (0, ei)),
        ],
        out_specs=pl.BlockSpec((tt, d), lambda ti, ei: (ti, 0)),
        out_shape=jax.ShapeDtypeStruct((n, d), F32),
        scratch_shapes=[
            pltpu.VMEM((d, tt), BF16),
            pltpu.VMEM((d, tt), F32),
            pltpu.VMEM((hp, nk, tt), F32),
            pltpu.VMEM((hp, nk, tt), F32),
            pltpu.VMEM((hp, tt), F32),
            pltpu.VMEM((2 * topk, tt), F32),
            pltpu.VMEM((n_cand_rows, tt), F32),
        ],
        compiler_params=pltpu.CompilerParams(
            dimension_semantics=("arbitrary", "arbitrary"), vmem_limit_bytes=VMEM_LIMIT),
        name="peer",
    )(x1, p["n2g"], p["wpqt"], p["keys"], p["u"], p["vt"])


def _prepare(norm1_g, w_in, conv_w, a_log, dt_bias, gdn_norm_g, w_gdn_out, cq_norm_g, w_uq, ckv_norm_g,
             w_ukv, q_norm_g, k_norm_g, w_mla_out, w_o, norm2_g, w_pq, sub_keys, u_tab, v_tab):
    d = w_in.shape[1]
    hg = a_log.shape[-1]
    dv = gdn_norm_g.shape[-1]
    v_w = w_gdn_out.shape[1]
    qkv_w = conv_w.shape[-1]
    qk_w = (qkv_w - v_w) // 2
    dk = qk_w // hg
    q_lora = cq_norm_g.shape[-1]
    kv_lora = ckv_norm_g.shape[-1]
    mla_qk = q_norm_g.shape[-1]
    hm = w_uq.shape[-1] // mla_qk
    mla_v = w_mla_out.shape[1] // hm
    nope = w_ukv.shape[-1] // hm - mla_v
    rope = mla_qk - nope
    assert mla_qk <= LANES and 2 * hg <= LANES and dk == dv

    win = w_in[0]
    o = 0
    cols = {}
    for name, width in (("qkv", qkv_w), ("z", v_w), ("b", hg), ("a", hg), ("cq", q_lora), ("ckv", kv_lora),
                        ("krot", rope), ("gates", 2 * d)):
        cols[name] = win[:, o:o + width]
        o += width
    assert o == win.shape[1]

    p = dict(n_gdn_heads=hg, dk=dk, dv=dv, qk_w=qk_w, v_w=v_w, n_mla_heads=hm, mla_qk=mla_qk, mla_v=mla_v,
             nope=nope, rope=rope, conv_k=conv_w.shape[1])
    p["n1g"] = norm1_g[0][None, :]
    p["wqkv"] = cols["qkv"].astype(BF16)
    p["convw"] = conv_w[0]
    head_of = np.arange(qk_w) // dk
    p["seg"] = jnp.asarray((head_of[:, None] == head_of[None, :]).astype(np.float32)).astype(BF16)
    p["wz"] = cols["z"].astype(BF16)
    wbg = jnp.zeros((d, LANES), F32).at[:, :hg].set(cols["b"]).at[:, hg:2 * hg].set(cols["a"])
    p["wbg"] = wbg.astype(BF16)
    nexp_a = jnp.zeros((LANES,), F32).at[hg:2 * hg].set(-jnp.exp(a_log[0]))
    dtb = jnp.zeros((LANES,), F32).at[hg:2 * hg].set(dt_bias[0])
    p["bgpar"] = jnp.stack([nexp_a, dtb])
    p["wcq"] = cols["cq"].astype(BF16)
    p["cqg"] = cq_norm_g[0][None, :]
    wuq = w_uq[0].reshape(q_lora, hm, mla_qk)
    p["wuq"] = jnp.pad(wuq, ((0, 0), (0, 0), (0, LANES - mla_qk))).reshape(q_lora, hm * LANES).astype(BF16)
    p["qng"] = jnp.pad(q_norm_g[0] * (mla_qk ** -0.5 * np.log2(np.e)), (0, LANES - mla_qk))[None, :]
    p["wckv"] = cols["ckv"].astype(BF16)
    p["ckvg"] = ckv_norm_g[0][None, :]
    wukv = w_ukv[0].reshape(kv_lora, hm, nope + mla_v)
    p["wuk"] = jnp.pad(wukv[:, :, :nope], ((0, 0), (0, 0), (0, LANES - nope))).reshape(kv_lora, hm * LANES).astype(BF16)
    p["wuvt"] = wukv[:, :, nope:].reshape(kv_lora, hm * mla_v).T.astype(BF16)
    p["wkrot"] = jnp.zeros((d, LANES), F32).at[:, nope:nope + rope].set(cols["krot"]).astype(BF16)
    p["kng"] = jnp.pad(k_norm_g[0], (0, LANES - mla_qk))[None, :]
    half = rope // 2
    inv = np.power(ROPE_THETA, -np.arange(half, dtype=np.float32) / half).astype(np.float32)
    invf = np.zeros((LANES,), np.float32)
    invf[nope:nope + half] = inv
    invf[nope + half:nope + rope] = inv
    sign = np.zeros((LANES,), np.float32)
    sign[nope:nope + half] = -1.0
    sign[nope + half:nope + rope] = 1.0
    p["ropepar"] = jnp.asarray(np.stack([invf, sign]))
    p["wg"] = cols["gates"].astype(BF16)

    p["gng"] = gdn_norm_g[0][None, :]
    p["wgo"] = w_gdn_out[0].astype(BF16)
    p["wmo"] = w_mla_out[0].astype(BF16)
    p["wo"] = w_o[0].astype(BF16)

    hp, _, nk, dkey = sub_keys.shape[1:]
    p["n_peer_heads"] = hp
    p["n_keys"] = nk
    assert nk == LANES and dkey == LANES
    p["n2g"] = norm2_g[0][None, :]
    p["wpqt"] = w_pq[0].T.astype(BF16)
    p["keys"] = sub_keys[0].reshape(hp * 2 * nk, dkey).astype(BF16)
    p["u"] = u_tab[0].astype(BF16)
    p["vt"] = v_tab[0].T.astype(BF16)
    return p


def kernel(x, positions, norm1_g, w_in, conv_w, a_log, dt_bias, gdn_norm_g, w_gdn_out, cq_norm_g, w_uq,
           ckv_norm_g, w_ukv, q_norm_g, k_norm_g, w_mla_out, w_o, norm2_g, w_pq, sub_keys, u_tab, v_tab):
    assert w_in.shape[0] == 1, "single-layer block"
    b, s, d = x.shape
    p = _prepare(norm1_g, w_in, conv_w, a_log, dt_bias, gdn_norm_g, w_gdn_out, cq_norm_g, w_uq, ckv_norm_g,
                 w_ukv, q_norm_g, k_norm_g, w_mla_out, w_o, norm2_g, w_pq, sub_keys, u_tab, v_tab)
    tm = min(256, s)
    gq, gk, gv, sz, bgc, sg, qp, kp, vt = _stage_inproj(x, positions, p, tm=tm)
    oa = _stage_gdn(gq, gk, gv, bgc, sz, p, b=b, s=s, ct=min(256, s))
    att = _stage_attn(qp, kp, vt, blk=min(512, s))
    x1 = _stage_merge(x.reshape(b * s, d), oa, att.reshape(b * s, -1), sg, p, tm=min(512, s))
    out = _stage_peer(x1, p, tt=min(512, b * s), ec=1024)
    return out.reshape(b, s, d)
```

```python
import functools

import jax
import jax.numpy as jnp
import numpy as np
from jax import lax
from jax.experimental import pallas as pl
from jax.experimental.pallas import tpu as pltpu

F32 = jnp.float32
BF16 = jnp.bfloat16

CHUNK = 64
NORM_EPS = 1e-6
ROPE_THETA = 10000.0
PEER_TOPK = 16
LANES = 128
PACK = 16
VMEM_LIMIT = 56 * 1024 * 1024

NT_DIMS = (((1,), (1,)), ((), ()))
TN_DIMS = (((0,), (0,)), ((), ()))


def _dot(a, b):
    return jnp.dot(a, b, preferred_element_type=F32)


def _dot_nt(a, b):
    return lax.dot_general(a, b, NT_DIMS, preferred_element_type=F32)


def _dot_tn(a, b):
    return lax.dot_general(a, b, TN_DIMS, preferred_element_type=F32)


def _rms(v, g):
    ms = jnp.mean(v * v, axis=-1, keepdims=True)
    return v * lax.rsqrt(ms + NORM_EPS) * g


def _sigmoid(v):
    return 1.0 / (1.0 + jnp.exp(-v))


def _softplus(v):
    return jnp.maximum(v, 0.0) + jnp.log(1.0 + jnp.exp(-jnp.abs(v)))


def _full(shape):
    n = len(shape)
    return pl.BlockSpec(shape, lambda *_: (0,) * n)


def _inproj_body(x_ref, xprev_ref, pos_ref, n1g_ref, wqkv_ref, convw_ref, seg_ref, wz_ref,
                 wbg_ref, bgpar_ref, wcq_ref, cqg_ref, wuq_ref, qng_ref, wckv_ref, ckvg_ref,
                 wuk_ref, wuvt_ref, wkrot_ref, kng_ref, ropepar_ref, wg_ref,
                 gq_ref, gk_ref, gv_ref, sz_ref, bgc_ref, sg_ref, qp_ref, kp_ref, vt_ref,
                 *, tm, qk_w, dk, n_heads, d_rot_lo, d_rot_half, d_qk, dv_mla, conv_k):
    s = pl.program_id(1)
    n1g = n1g_ref[...]
    h = _rms(x_ref[0], n1g).astype(BF16)
    hp = _rms(xprev_ref[0, 0], n1g).astype(BF16)

    pre = _dot(h, wqkv_ref[...])
    pre_prev = _dot(hp, wqkv_ref[...])
    pre_prev = jnp.where(s == 0, 0.0, pre_prev)
    ext = jnp.concatenate([pre_prev, pre], axis=0)
    cw = convw_ref[...]
    conv = cw[0:1, :] * ext[8 - conv_k + 1:8 - conv_k + 1 + tm, :]
    for k in range(1, conv_k):
        off = 8 - conv_k + 1 + k
        conv = conv + cw[k:k + 1, :] * ext[off:off + tm, :]
    act = conv * _sigmoid(conv)
    seg = seg_ref[...]

    def l2n(t):
        ss = _dot((t * t).astype(BF16), seg)
        return t * lax.rsqrt(ss + NORM_EPS)

    gq_ref[...] = (l2n(act[:, :qk_w]) * (dk ** -0.5)).astype(BF16)
    gk_ref[...] = l2n(act[:, qk_w:2 * qk_w]).astype(BF16)
    gv_ref[...] = act[:, 2 * qk_w:].astype(BF16)

    z = _dot(h, wz_ref[...])
    sz_ref[...] = (z * _sigmoid(z)).astype(BF16)

    raw = _dot(h, wbg_ref[...])
    lane = lax.broadcasted_iota(jnp.int32, raw.shape, 1)
    g = bgpar_ref[0:1, :] * _softplus(raw + bgpar_ref[1:2, :])
    bgc_ref[...] = jnp.where(lane < n_heads, _sigmoid(raw), g)

    pos = pos_ref[0].astype(F32)
    ang = pos * ropepar_ref[0:1, :]
    cosv = jnp.cos(ang)
    sinv = jnp.sin(ang) * ropepar_ref[1:2, :]
    lane128 = lax.broadcasted_iota(jnp.int32, ang.shape, 1)
    first_half = lane128 < d_rot_lo + d_rot_half

    def rope(y):
        partner = jnp.where(first_half, pltpu.roll(y, LANES - d_rot_half, 1), pltpu.roll(y, d_rot_half, 1))
        return y * cosv + partner * sinv

    def head_norm(slab, gain):
        ms = jnp.sum(slab * slab, axis=-1, keepdims=True) * (1.0 / d_qk)
        return slab * lax.rsqrt(ms + NORM_EPS) * gain

    cqn = _rms(_dot(h, wcq_ref[...]), cqg_ref[...]).astype(BF16)
    qall = _dot(cqn, wuq_ref[...])
    ckvn = _rms(_dot(h, wckv_ref[...]), ckvg_ref[...]).astype(BF16)
    kall = _dot(ckvn, wuk_ref[...])
    krot = _dot(h, wkrot_ref[...])
    qng = qng_ref[...]
    kng = kng_ref[...]
    for hh in range(n_heads):
        sl = slice(hh * LANES, (hh + 1) * LANES)
        qp_ref[0, hh] = rope(head_norm(qall[:, sl], qng)).astype(BF16)
        kp_ref[0, hh] = rope(head_norm(kall[:, sl] + krot, kng)).astype(BF16)
    vt = _dot_nt(wuvt_ref[...], ckvn)
    for hh in range(n_heads):
        vt_ref[0, hh] = vt[hh * dv_mla:(hh + 1) * dv_mla, :].astype(BF16)

    sg_ref[...] = _sigmoid(_dot(h, wg_ref[...])).astype(BF16)


def _stage_inproj(x, positions, p, *, tm):
    b, s, d = x.shape
    n = b * s
    nt = s // tm
    hm = p["n_mla_heads"]
    qk_w = p["qk_w"]
    body = functools.partial(
        _inproj_body, tm=tm, qk_w=qk_w, dk=p["dk"], n_heads=p["n_gdn_heads"],
        d_rot_lo=p["nope"], d_rot_half=p["rope"] // 2, d_qk=p["mla_qk"], dv_mla=p["mla_v"],
        conv_k=p["conv_k"])
    weights = [p["n1g"], p["wqkv"], p["convw"], p["seg"], p["wz"], p["wbg"], p["bgpar"], p["wcq"],
               p["cqg"], p["wuq"], p["qng"], p["wckv"], p["ckvg"], p["wuk"], p["wuvt"], p["wkrot"],
               p["kng"], p["ropepar"], p["wg"]]
    flat = lambda w: pl.BlockSpec((tm, w), lambda bi, si: (bi * nt + si, 0))
    in_specs = [
        pl.BlockSpec((1, tm, d), lambda bi, si: (bi, si, 0)),
        pl.BlockSpec((1, 1, 8, d), lambda bi, si: (bi, jnp.maximum(si * (tm // 8) - 1, 0), 0, 0)),
        pl.BlockSpec((1, tm, 1), lambda bi, si: (bi, si, 0)),
    ] + [_full(w.shape) for w in weights]
    v_w = p["v_w"]
    out_shape = [
        jax.ShapeDtypeStruct((n, qk_w), BF16), jax.ShapeDtypeStruct((n, qk_w), BF16),
        jax.ShapeDtypeStruct((n, v_w), BF16), jax.ShapeDtypeStruct((n, v_w), BF16),
        jax.ShapeDtypeStruct((n, LANES), F32), jax.ShapeDtypeStruct((n, 2 * d), BF16),
        jax.ShapeDtypeStruct((b, hm, s, LANES), BF16), jax.ShapeDtypeStruct((b, hm, s, LANES), BF16),
        jax.ShapeDtypeStruct((b, hm, p["mla_v"], s), BF16),
    ]
    out_specs = [
        flat(qk_w), flat(qk_w), flat(v_w), flat(v_w), flat(LANES), flat(2 * d),
        pl.BlockSpec((1, hm, tm, LANES), lambda bi, si: (bi, 0, si, 0)),
        pl.BlockSpec((1, hm, tm, LANES), lambda bi, si: (bi, 0, si, 0)),
        pl.BlockSpec((1, hm, p["mla_v"], tm), lambda bi, si: (bi, 0, 0, si)),
    ]
    return pl.pallas_call(
        body, grid=(b, nt), in_specs=in_specs, out_specs=out_specs, out_shape=out_shape,
        compiler_params=pltpu.CompilerParams(
            dimension_semantics=("arbitrary", "arbitrary"), vmem_limit_bytes=VMEM_LIMIT),
        name="inproj",
    )(x, x.reshape(b, s // 8, 8, d), positions.reshape(b, s, 1), *weights)


def _gdn_body(q_ref, k_ref, v_ref, bg_ref, sz_ref, gng_ref, o_ref, state_ref, *, n_chunks, n_heads, dk, dv):
    c = CHUNK

    @pl.when(pl.program_id(1) == 0)
    def _():
        state_ref[...] = jnp.zeros_like(state_ref)

    r_io = lax.broadcasted_iota(jnp.int32, (c, c), 0)
    c_io = lax.broadcasted_iota(jnp.int32, (c, c), 1)
    causal = r_io >= c_io
    strict = r_io > c_io
    tri_incl = causal.astype(F32)
    tri_upper = (r_io <= c_io).astype(F32)
    eye = (r_io == c_io).astype(F32)
    same16 = (r_io // 16) == (c_io // 16)
    same32 = (r_io // 32) == (c_io // 32)
    gng = gng_ref[...]

    def mm(a, b):
        return _dot(a.astype(BF16), b.astype(BF16))

    def chunk_step(ci, carry):
        rows = pl.ds(pl.multiple_of(ci * c, c), c)
        bg = bg_ref[rows, :]
        gc_cols = jnp.dot(tri_incl, bg, preferred_element_type=F32, precision=lax.Precision.HIGHEST)
        gc_rows = lax.dot_general(bg, tri_upper, TN_DIMS, preferred_element_type=F32,
                                  precision=lax.Precision.HIGHEST)
        q_all = q_ref[rows, :].astype(F32)
        k_all = k_ref[rows, :].astype(F32)
        v_all = v_ref[rows, :].astype(F32)
        sz_all = sz_ref[rows, :].astype(F32)
        states = [state_ref[hh] for hh in range(n_heads)]
        heads = range(n_heads)
        qs = [q_all[:, hh * dk:(hh + 1) * dk] for hh in heads]
        ks = [k_all[:, hh * dk:(hh + 1) * dk] for hh in heads]
        vs = [v_all[:, hh * dv:(hh + 1) * dv] for hh in heads]
        betas = [bg[:, hh:hh + 1] for hh in heads]
        gccs = [gc_cols[:, n_heads + hh:n_heads + hh + 1] for hh in heads]
        gcrs = [gc_rows[n_heads + hh:n_heads + hh + 1, :] for hh in heads]
        glasts = [g[c - 1:c, :] for g in gccs]
        decs = [jnp.where(causal, jnp.exp(jnp.minimum(gccs[hh] - gcrs[hh], 0.0)), 0.0) for hh in heads]
        kbs = [ks[hh] * betas[hh] for hh in heads]
        egs = [jnp.exp(g) for g in gccs]
        akqs = [_dot_nt(jnp.concatenate([kbs[hh], qs[hh]], axis=0).astype(BF16), ks[hh].astype(BF16))
                for hh in heads]
        lowers = [jnp.where(strict, akqs[hh][:c] * decs[hh], 0.0) for hh in heads]
        attns = [akqs[hh][c:] * decs[hh] for hh in heads]
        npows = [jnp.where(same16, -lo, 0.0) for lo in lowers]
        ts = [eye + nd for nd in npows]
        for _ in range(3):
            npows = [mm(a, a) for a in npows]
            ts = [ts[hh] + mm(ts[hh], npows[hh]) for hh in heads]
        off32 = jnp.logical_and(same32, jnp.logical_not(same16))
        tb = [mm(ts[hh], jnp.where(off32, lowers[hh], 0.0)) for hh in heads]
        ts = [ts[hh] - mm(tb[hh], ts[hh]) for hh in heads]
        tb = [mm(ts[hh], jnp.where(same32, 0.0, lowers[hh])) for hh in heads]
        ts = [ts[hh] - mm(tb[hh], ts[hh]) for hh in heads]
        uws = [mm(ts[hh], jnp.concatenate([vs[hh] * betas[hh], kbs[hh] * egs[hh]], axis=1))
               for hh in heads]
        wqss = [mm(jnp.concatenate([uws[hh][:, dv:], qs[hh] * egs[hh]], axis=0), states[hh])
                for hh in heads]
        v_news = [uws[hh][:, :dv] - wqss[hh][:c] for hh in heads]
        os_ = [wqss[hh][c:] + mm(attns[hh], v_news[hh]) for hh in heads]
        kdecs = [ks[hh] * jnp.exp(glasts[hh] - gccs[hh]) for hh in heads]
        new_states = [states[hh] * jnp.exp(glasts[hh]) + _dot_tn(kdecs[hh].astype(BF16), v_news[hh].astype(BF16))
                      for hh in heads]
        outs = [o * lax.rsqrt(jnp.mean(o * o, axis=-1, keepdims=True) + NORM_EPS) * gng for o in os_]
        o_ref[rows, :] = (jnp.concatenate(outs, axis=1) * sz_all).astype(BF16)
        for hh in range(n_heads):
            state_ref[hh] = new_states[hh]
        return carry

    lax.fori_loop(0, n_chunks, chunk_step, 0)


def _stage_gdn(gq, gk, gv, bgc, sz, p, *, b, s, ct):
    n = b * s
    nt = s // ct
    hg, dk, dv = p["n_gdn_heads"], p["dk"], p["dv"]
    body = functools.partial(_gdn_body, n_chunks=ct // CHUNK, n_heads=hg, dk=dk, dv=dv)
    flat = lambda w: pl.BlockSpec((ct, w), lambda bi, si: (bi * nt + si, 0))
    return pl.pallas_call(
        body, grid=(b, nt),
        in_specs=[flat(hg * dk), flat(hg * dk), flat(hg * dv), flat(LANES), flat(hg * dv), _full(p["gng"].shape)],
        out_specs=flat(hg * dv),
        out_shape=jax.ShapeDtypeStruct((n, hg * dv), BF16),
        scratch_shapes=[pltpu.VMEM((hg, dk, dv), F32)],
        compiler_params=pltpu.CompilerParams(
            dimension_semantics=("arbitrary", "arbitrary"), vmem_limit_bytes=VMEM_LIMIT),
        name="gdn",
    )(gq, gk, gv, bgc, sz, p["gng"])


def _attn_body(itab_ref, jtab_ref, q_ref, k_ref, vt_ref, o_ref, acc_ref, m_ref, l_ref, *, n_heads, dv, bq, bk):
    pidx = pl.program_id(1)
    i = itab_ref[pidx]
    j = jtab_ref[pidx]

    @pl.when(j == 0)
    def _():
        acc_ref[...] = jnp.zeros_like(acc_ref)
        m_ref[...] = jnp.full_like(m_ref, -jnp.inf)
        l_ref[...] = jnp.zeros_like(l_ref)

    def step(masked):
        if masked:
            kc = (j * bk + lax.broadcasted_iota(jnp.int32, (bk, bq), 0)) // CHUNK
            qc = (i * bq + lax.broadcasted_iota(jnp.int32, (bk, bq), 1)) // CHUNK
            allowed = kc <= qc
        def scores(hh):
            return _dot_nt(k_ref[0, hh], q_ref[0, hh])

        st_next = scores(0)
        for hh in range(n_heads):
            st = st_next
            if hh + 1 < n_heads:
                st_next = scores(hh + 1)
            if masked:
                st = jnp.where(allowed, st, -jnp.inf)
            m_prev = m_ref[hh:hh + 1, :]
            m_new = jnp.maximum(m_prev, jnp.max(st, axis=0, keepdims=True))
            alpha = jnp.exp2(m_prev - m_new)
            pt = jnp.exp2(st - m_new)
            l_ref[hh:hh + 1, :] = alpha * l_ref[hh:hh + 1, :] + jnp.sum(pt, axis=0, keepdims=True)
            m_ref[hh:hh + 1, :] = m_new
            rows = slice(hh * dv, (hh + 1) * dv)
            acc_ref[rows, :] = acc_ref[rows, :] * alpha + _dot(vt_ref[0, hh], pt.astype(BF16))

    @pl.when(j < i)
    def _():
        step(False)

    @pl.when(j == i)
    def _():
        step(True)
        inv = 1.0 / l_ref[...]
        for hh in range(n_heads):
            rows = slice(hh * dv, (hh + 1) * dv)
            acc_ref[rows, :] = acc_ref[rows, :] * inv[hh:hh + 1, :]
        o_ref[0] = acc_ref[...].T.astype(BF16)


def _stage_attn(qp, kp, vt, *, blk):
    b, hm, s, _ = qp.shape
    dv = vt.shape[2]
    nb = s // blk
    pairs = [(i, j) for i in range(nb) for j in range(i + 1)]
    itab = jnp.asarray(np.array([pq[0] for pq in pairs], np.int32))
    jtab = jnp.asarray(np.array([pq[1] for pq in pairs], np.int32))
    body = functools.partial(_attn_body, n_heads=hm, dv=dv, bq=blk, bk=blk)
    grid_spec = pltpu.PrefetchScalarGridSpec(
        num_scalar_prefetch=2, grid=(b, len(pairs)),
        in_specs=[
            pl.BlockSpec((1, hm, blk, LANES), lambda bi, pi, it, jt: (bi, 0, it[pi], 0)),
            pl.BlockSpec((1, hm, blk, LANES), lambda bi, pi, it, jt: (bi, 0, jt[pi], 0)),
            pl.BlockSpec((1, hm, dv, blk), lambda bi, pi, it, jt: (bi, 0, 0, jt[pi])),
        ],
        out_specs=pl.BlockSpec((1, blk, hm * dv), lambda bi, pi, it, jt: (bi, it[pi], 0)),
        scratch_shapes=[pltpu.VMEM((hm * dv, blk), F32), pltpu.VMEM((hm, blk), F32), pltpu.VMEM((hm, blk), F32)],
    )
    return pl.pallas_call(
        body, grid_spec=grid_spec,
        out_shape=jax.ShapeDtypeStruct((b, s, hm * dv), BF16),
        compiler_params=pltpu.CompilerParams(
            dimension_semantics=("arbitrary", "arbitrary"), vmem_limit_bytes=VMEM_LIMIT),
        name="attn",
    )(itab, jtab, qp, kp, vt)


def _merge_body(x_ref, oa_ref, att_ref, sg_ref, wgo_ref, wmo_ref, wo_ref, o_ref, *, d):
    ya = _dot(oa_ref[...], wgo_ref[...])
    yb = _dot(att_ref[...], wmo_ref[...])
    merged = sg_ref[:, :d].astype(F32) * ya + sg_ref[:, d:].astype(F32) * yb
    o_ref[...] = x_ref[...] + _dot(merged.astype(BF16), wo_ref[...])


def _stage_merge(x2, oa, att, sg, p, *, tm):
    n, d = x2.shape
    row = lambda w: pl.BlockSpec((tm, w), lambda i: (i, 0))
    return pl.pallas_call(
        functools.partial(_merge_body, d=d), grid=(n // tm,),
        in_specs=[row(d), row(oa.shape[1]), row(att.shape[1]), row(2 * d),
                  _full(p["wgo"].shape), _full(p["wmo"].shape), _full(p["wo"].shape)],
        out_specs=row(d), out_shape=jax.ShapeDtypeStruct((n, d), F32),
        compiler_params=pltpu.CompilerParams(dimension_semantics=("arbitrary",), vmem_limit_bytes=VMEM_LIMIT),
        name="merge",
    )(x2, oa, att, sg, p["wgo"], p["wmo"], p["wo"])


def _sort_desc(vals):
    n = len(vals)
    vals = list(vals)
    p = 1
    while p < n:
        k = p
        while k >= 1:
            for j in range(k % p, n - k, 2 * k):
                for i in range(min(k, n - j - k)):
                    if (i + j) // (2 * p) == (i + j + k) // (2 * p):
                        a, b = vals[i + j], vals[i + j + k]
                        vals[i + j], vals[i + j + k] = jnp.maximum(a, b), jnp.minimum(a, b)
            k //= 2
        p *= 2
    return vals


def _two_gelu(v):
    return v * (1.0 + lax.erf(v * (2.0 ** -0.5)))


def _peer_body(x_ref, n2g_ref, wpqt_ref, keys_ref, u_ref, vt_ref, o_ref,
               ht_ref, yt_ref, sc_ref, e2_ref, e1_ref, thr_ref, top_ref, cand_ref,
               *, tt, n_heads, n_keys, topk, ec, cand_pairs):
    e = pl.program_id(1)
    n_chunks = pl.num_programs(1)
    neg_inf = -jnp.inf

    @pl.when(e == 0)
    def _():
        h2 = _rms(x_ref[...], n2g_ref[...])
        ht = h2.T.astype(BF16)
        ht_ref[...] = ht
        yt_ref[...] = jnp.zeros_like(yt_ref)
        qt = _dot(wpqt_ref[...], ht).astype(BF16)
        for hh in range(n_heads):
            for half in range(2):
                r0 = (hh * 2 + half) * n_keys
                sc = _dot(keys_ref[r0:r0 + n_keys, :], qt[r0:r0 + n_keys, :])
                sc_ref[half] = sc
                cols = _sort_desc([sc[8 * g:8 * g + 8, :] for g in range(n_keys // 8)])[:topk]
                for r in range(topk):
                    mx = jnp.max(cols[0], axis=0, keepdims=True)
                    top_ref[half * topk + r:half * topk + r + 1, :] = mx
                    hit = cols[0] == mx
                    cols = [jnp.where(hit, cols[k + 1], cols[k]) for k in range(topk - r - 1)]
            for ci, (r1, r2) in enumerate(cand_pairs):
                cand_ref[ci:ci + 1, :] = top_ref[r1:r1 + 1, :] + top_ref[topk + r2:topk + r2 + 1, :]
            n_cand = len(cand_pairs)
            pad = cand_ref.shape[0] - n_cand
            if pad:
                cand_ref[n_cand:, :] = jnp.full((pad, tt), neg_inf, F32)
            work = cand_ref[...]
            smax = top_ref[0:1, :] + top_ref[topk:topk + 1, :]
            zsum = jnp.zeros((1, tt), F32)
            kth = smax
            for r in range(topk + 1):
                mx = jnp.max(work, axis=0, keepdims=True)
                if r < topk:
                    zsum = zsum + jnp.exp(mx - smax)
                    kth = mx
                else:
                    thr = 0.5 * (kth + mx)
                work = jnp.where(work == mx, neg_inf, work)
            half_inv_z = 0.5 / zsum
            e1_ref[hh] = jnp.exp(sc_ref[0] - top_ref[0:1, :]) * half_inv_z
            e2 = jnp.exp(sc_ref[1] - top_ref[topk:topk + 1, :])
            e2_ref[hh] = e2.astype(BF16).reshape(n_keys // PACK, PACK, tt)
            thr_ref[hh] = jnp.broadcast_to(jnp.exp(thr - smax) * half_inv_z, (PACK, tt)).astype(BF16)

    n_il = ec // n_keys
    i0 = pl.multiple_of(e * n_il, n_il)
    e1_blk = [e1_ref[hh, pl.ds(i0, n_il), :] for hh in range(n_heads)]
    ht = ht_ref[...]

    def activation(il):
        return _dot(u_ref[il * n_keys:(il + 1) * n_keys, :], ht)

    act_next = activation(0)
    pws = []
    for il in range(n_il):
        act = act_next
        if il + 1 < n_il:
            act_next = activation(il + 1)
        if il % 2 == 0 and il >= 2:
            rows = slice((il - 2) * n_keys, il * n_keys)
            yt_ref[...] += _dot(vt_ref[:, rows], jnp.concatenate(pws[il - 2:il], axis=0))
        wsum = jnp.zeros((n_keys // PACK, PACK, tt), BF16)
        for hh in range(n_heads):
            e1_row = jnp.broadcast_to(e1_blk[hh][il:il + 1, :], (PACK, tt)).astype(BF16)
            gate = e1_row[None] * e2_ref[hh]
            wsum = wsum + jnp.where(gate >= thr_ref[hh][None], gate, jnp.zeros_like(gate))
        pw = _two_gelu(act).astype(BF16).reshape(n_keys // PACK, PACK, tt) * wsum
        pws.append(pw.reshape(n_keys, tt))
    rows = slice((n_il - 2) * n_keys, n_il * n_keys)
    yt_ref[...] += _dot(vt_ref[:, rows], jnp.concatenate(pws[n_il - 2:], axis=0))

    @pl.when(e == n_chunks - 1)
    def _():
        o_ref[...] = x_ref[...] + yt_ref[...].T


def _stage_peer(x1, p, *, tt, ec):
    n, d = x1.shape
    hp, nk, topk = p["n_peer_heads"], p["n_keys"], PEER_TOPK
    nc = p["u"].shape[0] // ec
    cand_pairs = tuple((a, c) for a in range(topk) for c in range(topk) if (a + 1) * (c + 1) <= topk)
    n_cand_rows = -(-len(cand_pairs) // 8) * 8
    body = functools.partial(_peer_body, tt=tt, n_heads=hp, n_keys=nk, topk=topk, ec=ec, cand_pairs=cand_pairs)
    return pl.pallas_call(
        body, grid=(n // tt, nc),
        in_specs=[
            pl.BlockSpec((tt, d), lambda ti, ei: (ti, 0)),
            _full(p["n2g"].shape), _full(p["wpqt"].shape), _full(p["keys"].shape),
            pl.BlockSpec((ec, d), lambda ti, ei: (ei, 0)),
            pl.BlockSpec((d, ec), lambda ti, ei: (0, ei)),
        ],
        out_specs=pl.BlockSpec((tt, d), lambda ti, ei: (ti, 0)),
        out_shape=jax.ShapeDtypeStruct((n, d), F32),
        scratch_shapes=[
            pltpu.VMEM((d, tt), BF16),
            pltpu.VMEM((d, tt), F32),
            pltpu.VMEM((2, nk, tt), F32),
            pltpu.VMEM((hp, nk // PACK, PACK, tt), BF16),
            pltpu.VMEM((hp, nk, tt), F32),
            pltpu.VMEM((hp, PACK, tt), BF16),
            pltpu.VMEM((2 * topk, tt), F32),
            pltpu.VMEM((n_cand_rows, tt), F32),
        ],
        compiler_params=pltpu.CompilerParams(
            dimension_semantics=("arbitrary", "arbitrary"), vmem_limit_bytes=VMEM_LIMIT),
        name="peer",
    )(x1, p["n2g"], p["wpqt"], p["keys"], p["u"], p["vt"])


def _prepare(norm1_g, w_in, conv_w, a_log, dt_bias, gdn_norm_g, w_gdn_out, cq_norm_g, w_uq, ckv_norm_g,
             w_ukv, q_norm_g, k_norm_g, w_mla_out, w_o, norm2_g, w_pq, sub_keys, u_tab, v_tab):
    d = w_in.shape[1]
    hg = a_log.shape[-1]
    dv = gdn_norm_g.shape[-1]
    v_w = w_gdn_out.shape[1]
    qkv_w = conv_w.shape[-1]
    qk_w = (qkv_w - v_w) // 2
    dk = qk_w // hg
    q_lora = cq_norm_g.shape[-1]
    kv_lora = ckv_norm_g.shape[-1]
    mla_qk = q_norm_g.shape[-1]
    hm = w_uq.shape[-1] // mla_qk
    mla_v = w_mla_out.shape[1] // hm
    nope = w_ukv.shape[-1] // hm - mla_v
    rope = mla_qk - nope
    assert mla_qk <= LANES and 2 * hg <= LANES and dk == dv

    win = w_in[0]
    o = 0
    cols = {}
    for name, width in (("qkv", qkv_w), ("z", v_w), ("b", hg), ("a", hg), ("cq", q_lora), ("ckv", kv_lora),
                        ("krot", rope), ("gates", 2 * d)):
        cols[name] = win[:, o:o + width]
        o += width
    assert o == win.shape[1]

    p = dict(n_gdn_heads=hg, dk=dk, dv=dv, qk_w=qk_w, v_w=v_w, n_mla_heads=hm, mla_qk=mla_qk, mla_v=mla_v,
             nope=nope, rope=rope, conv_k=conv_w.shape[1])
    p["n1g"] = norm1_g[0][None, :]
    p["wqkv"] = cols["qkv"].astype(BF16)
    p["convw"] = conv_w[0]
    head_of = np.arange(qk_w) // dk
    p["seg"] = jnp.asarray((head_of[:, None] == head_of[None, :]).astype(np.float32)).astype(BF16)
    p["wz"] = cols["z"].astype(BF16)
    wbg = jnp.zeros((d, LANES), F32).at[:, :hg].set(cols["b"]).at[:, hg:2 * hg].set(cols["a"])
    p["wbg"] = wbg.astype(BF16)
    nexp_a = jnp.zeros((LANES,), F32).at[hg:2 * hg].set(-jnp.exp(a_log[0]))
    dtb = jnp.zeros((LANES,), F32).at[hg:2 * hg].set(dt_bias[0])
    p["bgpar"] = jnp.stack([nexp_a, dtb])
    p["wcq"] = cols["cq"].astype(BF16)
    p["cqg"] = cq_norm_g[0][None, :]
    wuq = w_uq[0].reshape(q_lora, hm, mla_qk)
    p["wuq"] = jnp.pad(wuq, ((0, 0), (0, 0), (0, LANES - mla_qk))).reshape(q_lora, hm * LANES).astype(BF16)
    p["qng"] = jnp.pad(q_norm_g[0] * (mla_qk ** -0.5 * np.log2(np.e)), (0, LANES - mla_qk))[None, :]
    p["wckv"] = cols["ckv"].astype(BF16)
    p["ckvg"] = ckv_norm_g[0][None, :]
    wukv = w_ukv[0].reshape(kv_lora, hm, nope + mla_v)
    p["wuk"] = jnp.pad(wukv[:, :, :nope], ((0, 0), (0, 0), (0, LANES - nope))).reshape(kv_lora, hm * LANES).astype(BF16)
    p["wuvt"] = wukv[:, :, nope:].reshape(kv_lora, hm * mla_v).T.astype(BF16)
    p["wkrot"] = jnp.zeros((d, LANES), F32).at[:, nope:nope + rope].set(cols["krot"]).astype(BF16)
    p["kng"] = jnp.pad(k_norm_g[0], (0, LANES - mla_qk))[None, :]
    half = rope // 2
    inv = np.power(ROPE_THETA, -np.arange(half, dtype=np.float32) / half).astype(np.float32)
    invf = np.zeros((LANES,), np.float32)
    invf[nope:nope + half] = inv
    invf[nope + half:nope + rope] = inv
    sign = np.zeros((LANES,), np.float32)
    sign[nope:nope + half] = -1.0
    sign[nope + half:nope + rope] = 1.0
    p["ropepar"] = jnp.asarray(np.stack([invf, sign]))
    p["wg"] = cols["gates"].astype(BF16)

    p["gng"] = gdn_norm_g[0][None, :]
    p["wgo"] = w_gdn_out[0].astype(BF16)
    p["wmo"] = w_mla_out[0].astype(BF16)
    p["wo"] = w_o[0].astype(BF16)

    hp, _, nk, dkey = sub_keys.shape[1:]
    p["n_peer_heads"] = hp
    p["n_keys"] = nk
    assert nk == LANES and dkey == LANES
    p["n2g"] = norm2_g[0][None, :]
    p["wpqt"] = w_pq[0].T.astype(BF16)
    p["keys"] = sub_keys[0].reshape(hp * 2 * nk, dkey).astype(BF16)
    p["u"] = u_tab[0].astype(BF16)
    p["vt"] = v_tab[0].T.astype(BF16)
    return p


def kernel(x, positions, norm1_g, w_in, conv_w, a_log, dt_bias, gdn_norm_g, w_gdn_out, cq_norm_g, w_uq,
           ckv_norm_g, w_ukv, q_norm_g, k_norm_g, w_mla_out, w_o, norm2_g, w_pq, sub_keys, u_tab, v_tab):
    assert w_in.shape[0] == 1, "single-layer block"
    b, s, d = x.shape
    p = _prepare(norm1_g, w_in, conv_w, a_log, dt_bias, gdn_norm_g, w_gdn_out, cq_norm_g, w_uq, ckv_norm_g,
                 w_ukv, q_norm_g, k_norm_g, w_mla_out, w_o, norm2_g, w_pq, sub_keys, u_tab, v_tab)
    tm = min(256, s)
    gq, gk, gv, sz, bgc, sg, qp, kp, vt = _stage_inproj(x, positions, p, tm=tm)
    oa = _stage_gdn(gq, gk, gv, bgc, sz, p, b=b, s=s, ct=min(256, s))
    att = _stage_attn(qp, kp, vt, blk=min(512, s))
    x1 = _stage_merge(x.reshape(b * s, d), oa, att.reshape(b * s, -1), sg, p, tm=min(512, s))
    out = _stage_peer(x1, p, tt=min(512, b * s), ec=2048)
    return out.reshape(b, s, d)
```

```python
import functools

import jax
import jax.numpy as jnp
import numpy as np
from jax import lax
from jax.experimental import pallas as pl
from jax.experimental.pallas import tpu as pltpu

F32 = jnp.float32
BF16 = jnp.bfloat16

CHUNK = 64
NORM_EPS = 1e-6
ROPE_THETA = 10000.0
PEER_TOPK = 16
LANES = 128
PACK = 16
GDN_GROUP = 4
VMEM_LIMIT = 56 * 1024 * 1024

NT_DIMS = (((1,), (1,)), ((), ()))
TN_DIMS = (((0,), (0,)), ((), ()))


def _dot(a, b):
    return jnp.dot(a, b, preferred_element_type=F32)


def _dot_nt(a, b):
    return lax.dot_general(a, b, NT_DIMS, preferred_element_type=F32)


def _dot_tn(a, b):
    return lax.dot_general(a, b, TN_DIMS, preferred_element_type=F32)


def _rms(v, g):
    ms = jnp.mean(v * v, axis=-1, keepdims=True)
    return v * lax.rsqrt(ms + NORM_EPS) * g


def _sigmoid(v):
    return 1.0 / (1.0 + jnp.exp(-v))


def _softplus(v):
    return jnp.maximum(v, 0.0) + jnp.log(1.0 + jnp.exp(-jnp.abs(v)))


def _full(shape):
    n = len(shape)
    return pl.BlockSpec(shape, lambda *_: (0,) * n)


def _inproj_body(x_ref, xprev_ref, pos_ref, n1g_ref, wqkv_ref, convw_ref, seg_ref, wz_ref,
                 wbg_ref, bgpar_ref, wcq_ref, cqg_ref, wuq_ref, qng_ref, wckv_ref, ckvg_ref,
                 wuk_ref, wuvt_ref, wkrot_ref, kng_ref, ropepar_ref, wg_ref,
                 gq_ref, gk_ref, gv_ref, sz_ref, bgc_ref, sg_ref, qp_ref, kp_ref, vt_ref,
                 *, tm, qk_w, dk, n_heads, d_rot_lo, d_rot_half, d_qk, dv_mla, conv_k):
    s = pl.program_id(1)
    n1g = n1g_ref[...]
    h = _rms(x_ref[0], n1g).astype(BF16)
    hp = _rms(xprev_ref[0, 0], n1g).astype(BF16)

    pre = _dot(h, wqkv_ref[...])
    pre_prev = _dot(hp, wqkv_ref[...])
    pre_prev = jnp.where(s == 0, 0.0, pre_prev)
    ext = jnp.concatenate([pre_prev, pre], axis=0)
    cw = convw_ref[...]
    conv = cw[0:1, :] * ext[8 - conv_k + 1:8 - conv_k + 1 + tm, :]
    for k in range(1, conv_k):
        off = 8 - conv_k + 1 + k
        conv = conv + cw[k:k + 1, :] * ext[off:off + tm, :]
    act = conv * _sigmoid(conv)
    seg = seg_ref[...]

    def l2n(t):
        ss = _dot((t * t).astype(BF16), seg)
        return t * lax.rsqrt(ss + NORM_EPS)

    gq_ref[...] = (l2n(act[:, :qk_w]) * (dk ** -0.5)).astype(BF16)
    gk_ref[...] = l2n(act[:, qk_w:2 * qk_w]).astype(BF16)
    gv_ref[...] = act[:, 2 * qk_w:].astype(BF16)

    z = _dot(h, wz_ref[...])
    sz_ref[...] = (z * _sigmoid(z)).astype(BF16)

    raw = _dot(h, wbg_ref[...])
    lane = lax.broadcasted_iota(jnp.int32, raw.shape, 1)
    g = bgpar_ref[0:1, :] * _softplus(raw + bgpar_ref[1:2, :])
    bgc_ref[...] = jnp.where(lane < n_heads, _sigmoid(raw), g)

    pos = pos_ref[0].astype(F32)
    ang = pos * ropepar_ref[0:1, :]
    cosv = jnp.cos(ang)
    sinv = jnp.sin(ang) * ropepar_ref[1:2, :]
    lane128 = lax.broadcasted_iota(jnp.int32, ang.shape, 1)
    first_half = lane128 < d_rot_lo + d_rot_half

    def rope(y):
        partner = jnp.where(first_half, pltpu.roll(y, LANES - d_rot_half, 1), pltpu.roll(y, d_rot_half, 1))
        return y * cosv + partner * sinv

    def head_norm(slab, gain):
        ms = jnp.sum(slab * slab, axis=-1, keepdims=True) * (1.0 / d_qk)
        return slab * lax.rsqrt(ms + NORM_EPS) * gain

    cqn = _rms(_dot(h, wcq_ref[...]), cqg_ref[...]).astype(BF16)
    qall = _dot(cqn, wuq_ref[...])
    ckvn = _rms(_dot(h, wckv_ref[...]), ckvg_ref[...]).astype(BF16)
    kall = _dot(ckvn, wuk_ref[...])
    krot = _dot(h, wkrot_ref[...])
    qng = qng_ref[...]
    kng = kng_ref[...]
    for hh in range(n_heads):
        sl = slice(hh * LANES, (hh + 1) * LANES)
        qp_ref[0, hh] = rope(head_norm(qall[:, sl], qng)).astype(BF16)
        kp_ref[0, hh] = rope(head_norm(kall[:, sl] + krot, kng)).astype(BF16)
    vt = _dot_nt(wuvt_ref[...], ckvn)
    for hh in range(n_heads):
        vt_ref[0, hh] = vt[hh * dv_mla:(hh + 1) * dv_mla, :].astype(BF16)

    sg_ref[...] = _sigmoid(_dot(h, wg_ref[...])).astype(BF16)


def _stage_inproj(x, positions, p, *, tm):
    b, s, d = x.shape
    n = b * s
    nt = s // tm
    hm = p["n_mla_heads"]
    qk_w = p["qk_w"]
    body = functools.partial(
        _inproj_body, tm=tm, qk_w=qk_w, dk=p["dk"], n_heads=p["n_gdn_heads"],
        d_rot_lo=p["nope"], d_rot_half=p["rope"] // 2, d_qk=p["mla_qk"], dv_mla=p["mla_v"],
        conv_k=p["conv_k"])
    weights = [p["n1g"], p["wqkv"], p["convw"], p["seg"], p["wz"], p["wbg"], p["bgpar"], p["wcq"],
               p["cqg"], p["wuq"], p["qng"], p["wckv"], p["ckvg"], p["wuk"], p["wuvt"], p["wkrot"],
               p["kng"], p["ropepar"], p["wg"]]
    flat = lambda w: pl.BlockSpec((tm, w), lambda bi, si: (bi * nt + si, 0))
    in_specs = [
        pl.BlockSpec((1, tm, d), lambda bi, si: (bi, si, 0)),
        pl.BlockSpec((1, 1, 8, d), lambda bi, si: (bi, jnp.maximum(si * (tm // 8) - 1, 0), 0, 0)),
        pl.BlockSpec((1, tm, 1), lambda bi, si: (bi, si, 0)),
    ] + [_full(w.shape) for w in weights]
    v_w = p["v_w"]
    out_shape = [
        jax.ShapeDtypeStruct((n, qk_w), BF16), jax.ShapeDtypeStruct((n, qk_w), BF16),
        jax.ShapeDtypeStruct((n, v_w), BF16), jax.ShapeDtypeStruct((n, v_w), BF16),
        jax.ShapeDtypeStruct((n, LANES), F32), jax.ShapeDtypeStruct((n, 2 * d), BF16),
        jax.ShapeDtypeStruct((b, hm, s, LANES), BF16), jax.ShapeDtypeStruct((b, hm, s, LANES), BF16),
        jax.ShapeDtypeStruct((b, hm, p["mla_v"], s), BF16),
    ]
    out_specs = [
        flat(qk_w), flat(qk_w), flat(v_w), flat(v_w), flat(LANES), flat(2 * d),
        pl.BlockSpec((1, hm, tm, LANES), lambda bi, si: (bi, 0, si, 0)),
        pl.BlockSpec((1, hm, tm, LANES), lambda bi, si: (bi, 0, si, 0)),
        pl.BlockSpec((1, hm, p["mla_v"], tm), lambda bi, si: (bi, 0, 0, si)),
    ]
    return pl.pallas_call(
        body, grid=(b, nt), in_specs=in_specs, out_specs=out_specs, out_shape=out_shape,
        compiler_params=pltpu.CompilerParams(
            dimension_semantics=("arbitrary", "arbitrary"), vmem_limit_bytes=VMEM_LIMIT),
        name="inproj",
    )(x, x.reshape(b, s // 8, 8, d), positions.reshape(b, s, 1), *weights)


def _gdn_body(q_ref, k_ref, v_ref, bg_ref, sz_ref, gng_ref, o_ref, state_ref, *, n_chunks, n_heads, dk, dv):
    c = CHUNK

    @pl.when(pl.program_id(1) == 0)
    def _():
        state_ref[...] = jnp.zeros_like(state_ref)

    r_io = lax.broadcasted_iota(jnp.int32, (c, c), 0)
    c_io = lax.broadcasted_iota(jnp.int32, (c, c), 1)
    causal = r_io >= c_io
    strict = r_io > c_io
    tri_incl = causal.astype(F32)
    tri_upper = (r_io <= c_io).astype(F32)
    eye = (r_io == c_io).astype(F32)
    same16 = (r_io // 16) == (c_io // 16)
    same32 = (r_io // 32) == (c_io // 32)
    gng = gng_ref[...]

    def mm(a, b):
        return _dot(a.astype(BF16), b.astype(BF16))

    def group_step(gi, carry):
        items = [(cc, hh) for cc in range(GDN_GROUP) for hh in range(n_heads)]
        rows = [pl.ds(pl.multiple_of((gi * GDN_GROUP + cc) * c, c), c) for cc in range(GDN_GROUP)]
        bgs = [bg_ref[r, :] for r in rows]
        gc_cols = [jnp.dot(tri_incl, bg, preferred_element_type=F32, precision=lax.Precision.HIGHEST)
                   for bg in bgs]
        gc_rows = [lax.dot_general(bg, tri_upper, TN_DIMS, preferred_element_type=F32,
                                   precision=lax.Precision.HIGHEST) for bg in bgs]
        q_all = [q_ref[r, :].astype(F32) for r in rows]
        k_all = [k_ref[r, :].astype(F32) for r in rows]
        v_all = [v_ref[r, :].astype(F32) for r in rows]
        sz_all = [sz_ref[r, :].astype(F32) for r in rows]
        states = [state_ref[hh] for hh in range(n_heads)]
        qs = [q_all[cc][:, hh * dk:(hh + 1) * dk] for cc, hh in items]
        ks = [k_all[cc][:, hh * dk:(hh + 1) * dk] for cc, hh in items]
        vs = [v_all[cc][:, hh * dv:(hh + 1) * dv] for cc, hh in items]
        betas = [bgs[cc][:, hh:hh + 1] for cc, hh in items]
        gccs = [gc_cols[cc][:, n_heads + hh:n_heads + hh + 1] for cc, hh in items]
        gcrs = [gc_rows[cc][n_heads + hh:n_heads + hh + 1, :] for cc, hh in items]
        n_it = range(len(items))
        glasts = [g[c - 1:c, :] for g in gccs]
        decs = [jnp.where(causal, jnp.exp(jnp.minimum(gccs[t] - gcrs[t], 0.0)), 0.0) for t in n_it]
        kbs = [ks[t] * betas[t] for t in n_it]
        egs = [jnp.exp(g) for g in gccs]
        akqs = [_dot_nt(jnp.concatenate([kbs[t], qs[t]], axis=0).astype(BF16), ks[t].astype(BF16))
                for t in n_it]
        lowers = [jnp.where(strict, akqs[t][:c] * decs[t], 0.0) for t in n_it]
        attns = [akqs[t][c:] * decs[t] for t in n_it]
        npows = [jnp.where(same16, -lo, 0.0) for lo in lowers]
        ts = [eye + nd for nd in npows]
        for _ in range(3):
            npows = [mm(a, a) for a in npows]
            ts = [ts[t] + mm(ts[t], npows[t]) for t in n_it]
        off32 = jnp.logical_and(same32, jnp.logical_not(same16))
        tb = [mm(ts[t], jnp.where(off32, lowers[t], 0.0)) for t in n_it]
        ts = [ts[t] - mm(tb[t], ts[t]) for t in n_it]
        tb = [mm(ts[t], jnp.where(same32, 0.0, lowers[t])) for t in n_it]
        ts = [ts[t] - mm(tb[t], ts[t]) for t in n_it]
        uws = [mm(ts[t], jnp.concatenate([vs[t] * betas[t], kbs[t] * egs[t]], axis=1))
               for t in n_it]
        wqgs = [jnp.concatenate([uws[t][:, dv:], qs[t] * egs[t]], axis=0) for t in n_it]
        kdecs = [(ks[t] * jnp.exp(glasts[t] - gccs[t])).astype(BF16) for t in n_it]
        for cc in range(GDN_GROUP):
            base = cc * n_heads
            heads = range(n_heads)
            wqss = [mm(wqgs[base + hh], states[hh]) for hh in heads]
            v_news = [uws[base + hh][:, :dv] - wqss[hh][:c] for hh in heads]
            os_ = [wqss[hh][c:] + mm(attns[base + hh], v_news[hh]) for hh in heads]
            states = [states[hh] * jnp.exp(glasts[base + hh]) + _dot_tn(kdecs[base + hh], v_news[hh].astype(BF16))
                      for hh in heads]
            outs = [o * lax.rsqrt(jnp.mean(o * o, axis=-1, keepdims=True) + NORM_EPS) * gng for o in os_]
            o_ref[rows[cc], :] = (jnp.concatenate(outs, axis=1) * sz_all[cc]).astype(BF16)
        for hh in range(n_heads):
            state_ref[hh] = states[hh]
        return carry

    lax.fori_loop(0, n_chunks // GDN_GROUP, group_step, 0)


def _stage_gdn(gq, gk, gv, bgc, sz, p, *, b, s, ct):
    n = b * s
    nt = s // ct
    hg, dk, dv = p["n_gdn_heads"], p["dk"], p["dv"]
    body = functools.partial(_gdn_body, n_chunks=ct // CHUNK, n_heads=hg, dk=dk, dv=dv)
    flat = lambda w: pl.BlockSpec((ct, w), lambda bi, si: (bi * nt + si, 0))
    return pl.pallas_call(
        body, grid=(b, nt),
        in_specs=[flat(hg * dk), flat(hg * dk), flat(hg * dv), flat(LANES), flat(hg * dv), _full(p["gng"].shape)],
        out_specs=flat(hg * dv),
        out_shape=jax.ShapeDtypeStruct((n, hg * dv), BF16),
        scratch_shapes=[pltpu.VMEM((hg, dk, dv), F32)],
        compiler_params=pltpu.CompilerParams(
            dimension_semantics=("arbitrary", "arbitrary"), vmem_limit_bytes=VMEM_LIMIT),
        name="gdn",
    )(gq, gk, gv, bgc, sz, p["gng"])


def _attn_body(itab_ref, jtab_ref, q_ref, k_ref, vt_ref, o_ref, acc_ref, m_ref, l_ref, *, n_heads, dv, bq, bk):
    pidx = pl.program_id(1)
    i = itab_ref[pidx]
    j = jtab_ref[pidx]

    @pl.when(j == 0)
    def _():
        acc_ref[...] = jnp.zeros_like(acc_ref)
        m_ref[...] = jnp.full_like(m_ref, -jnp.inf)
        l_ref[...] = jnp.zeros_like(l_ref)

    def step(masked):
        if masked:
            kc = (j * bk + lax.broadcasted_iota(jnp.int32, (bk, bq), 0)) // CHUNK
            qc = (i * bq + lax.broadcasted_iota(jnp.int32, (bk, bq), 1)) // CHUNK
            allowed = kc <= qc
        def scores(hh):
            return _dot_nt(k_ref[0, hh], q_ref[0, hh])

        def col_max(st, hh):
            if masked:
                st = jnp.where(allowed, st, -jnp.inf)
            return st, jnp.maximum(m_ref[hh:hh + 1, :], jnp.max(st, axis=0, keepdims=True))

        st_cur, m_cur = col_max(scores(0), 0)
        st_next = scores(1)
        for hh in range(n_heads):
            st, m_new = st_cur, m_cur
            if hh + 2 < n_heads:
                st_after = scores(hh + 2)
            if hh + 1 < n_heads:
                st_cur, m_cur = col_max(st_next, hh + 1)
                st_next = st_after if hh + 2 < n_heads else None
            m_prev = m_ref[hh:hh + 1, :]
            alpha = jnp.exp2(m_prev - m_new)
            pt = jnp.exp2(st - m_new)
            l_ref[hh:hh + 1, :] = alpha * l_ref[hh:hh + 1, :] + jnp.sum(pt, axis=0, keepdims=True)
            m_ref[hh:hh + 1, :] = m_new
            rows = slice(hh * dv, (hh + 1) * dv)
            acc_ref[rows, :] = acc_ref[rows, :] * alpha + _dot(vt_ref[0, hh], pt.astype(BF16))

    @pl.when(j < i)
    def _():
        step(False)

    @pl.when(j == i)
    def _():
        step(True)
        inv = 1.0 / l_ref[...]
        for hh in range(n_heads):
            rows = slice(hh * dv, (hh + 1) * dv)
            acc_ref[rows, :] = acc_ref[rows, :] * inv[hh:hh + 1, :]
        o_ref[0] = acc_ref[...].T.astype(BF16)


def _stage_attn(qp, kp, vt, *, blk):
    b, hm, s, _ = qp.shape
    dv = vt.shape[2]
    nb = s // blk
    pairs = [(i, j) for i in range(nb) for j in range(i + 1)]
    itab = jnp.asarray(np.array([pq[0] for pq in pairs], np.int32))
    jtab = jnp.asarray(np.array([pq[1] for pq in pairs], np.int32))
    body = functools.partial(_attn_body, n_heads=hm, dv=dv, bq=blk, bk=blk)
    grid_spec = pltpu.PrefetchScalarGridSpec(
        num_scalar_prefetch=2, grid=(b, len(pairs)),
        in_specs=[
            pl.BlockSpec((1, hm, blk, LANES), lambda bi, pi, it, jt: (bi, 0, it[pi], 0)),
            pl.BlockSpec((1, hm, blk, LANES), lambda bi, pi, it, jt: (bi, 0, jt[pi], 0)),
            pl.BlockSpec((1, hm, dv, blk), lambda bi, pi, it, jt: (bi, 0, 0, jt[pi])),
        ],
        out_specs=pl.BlockSpec((1, blk, hm * dv), lambda bi, pi, it, jt: (bi, it[pi], 0)),
        scratch_shapes=[pltpu.VMEM((hm * dv, blk), F32), pltpu.VMEM((hm, blk), F32), pltpu.VMEM((hm, blk), F32)],
    )
    return pl.pallas_call(
        body, grid_spec=grid_spec,
        out_shape=jax.ShapeDtypeStruct((b, s, hm * dv), BF16),
        compiler_params=pltpu.CompilerParams(
            dimension_semantics=("arbitrary", "arbitrary"), vmem_limit_bytes=VMEM_LIMIT),
        name="attn",
    )(itab, jtab, qp, kp, vt)


def _merge_body(x_ref, oa_ref, att_ref, sg_ref, wgo_ref, wmo_ref, wo_ref, o_ref, *, d):
    ya = _dot(oa_ref[...], wgo_ref[...])
    yb = _dot(att_ref[...], wmo_ref[...])
    merged = sg_ref[:, :d].astype(F32) * ya + sg_ref[:, d:].astype(F32) * yb
    o_ref[...] = x_ref[...] + _dot(merged.astype(BF16), wo_ref[...])


def _stage_merge(x2, oa, att, sg, p, *, tm):
    n, d = x2.shape
    row = lambda w: pl.BlockSpec((tm, w), lambda i: (i, 0))
    return pl.pallas_call(
        functools.partial(_merge_body, d=d), grid=(n // tm,),
        in_specs=[row(d), row(oa.shape[1]), row(att.shape[1]), row(2 * d),
                  _full(p["wgo"].shape), _full(p["wmo"].shape), _full(p["wo"].shape)],
        out_specs=row(d), out_shape=jax.ShapeDtypeStruct((n, d), F32),
        compiler_params=pltpu.CompilerParams(dimension_semantics=("arbitrary",), vmem_limit_bytes=VMEM_LIMIT),
        name="merge",
    )(x2, oa, att, sg, p["wgo"], p["wmo"], p["wo"])


def _sort_desc(vals):
    n = len(vals)
    vals = list(vals)
    p = 1
    while p < n:
        k = p
        while k >= 1:
            for j in range(k % p, n - k, 2 * k):
                for i in range(min(k, n - j - k)):
                    if (i + j) // (2 * p) == (i + j + k) // (2 * p):
                        a, b = vals[i + j], vals[i + j + k]
                        vals[i + j], vals[i + j + k] = jnp.maximum(a, b), jnp.minimum(a, b)
            k //= 2
        p *= 2
    return vals


def _two_gelu(v):
    return v * (1.0 + lax.erf(v * (2.0 ** -0.5)))


def _peer_body(x_ref, n2g_ref, wpqt_ref, keys_ref, u_ref, vt_ref, o_ref,
               ht_ref, yt_ref, sc_ref, e2_ref, e1_ref, thr_ref, top_ref, cand_ref,
               *, tt, n_heads, n_keys, topk, ec, cand_pairs):
    e = pl.program_id(1)
    n_chunks = pl.num_programs(1)
    neg_inf = -jnp.inf

    @pl.when(e == 0)
    def _():
        h2 = _rms(x_ref[...], n2g_ref[...])
        ht = h2.T.astype(BF16)
        ht_ref[...] = ht
        yt_ref[...] = jnp.zeros_like(yt_ref)
        qt = _dot(wpqt_ref[...], ht).astype(BF16)
        for hh in range(n_heads):
            for half in range(2):
                r0 = (hh * 2 + half) * n_keys
                sc = _dot(keys_ref[r0:r0 + n_keys, :], qt[r0:r0 + n_keys, :])
                sc_ref[half] = sc
                cols = _sort_desc([sc[8 * g:8 * g + 8, :] for g in range(n_keys // 8)])[:topk]
                for r in range(topk):
                    mx = jnp.max(cols[0], axis=0, keepdims=True)
                    top_ref[half * topk + r:half * topk + r + 1, :] = mx
                    hit = cols[0] == mx
                    cols = [jnp.where(hit, cols[k + 1], cols[k]) for k in range(topk - r - 1)]
            for ci, (r1, r2) in enumerate(cand_pairs):
                cand_ref[ci:ci + 1, :] = top_ref[r1:r1 + 1, :] + top_ref[topk + r2:topk + r2 + 1, :]
            n_cand = len(cand_pairs)
            pad = cand_ref.shape[0] - n_cand
            if pad:
                cand_ref[n_cand:, :] = jnp.full((pad, tt), neg_inf, F32)
            work = cand_ref[...]
            smax = top_ref[0:1, :] + top_ref[topk:topk + 1, :]
            zsum = jnp.zeros((1, tt), F32)
            kth = smax
            for r in range(topk + 1):
                mx = jnp.max(work, axis=0, keepdims=True)
                if r < topk:
                    zsum = zsum + jnp.exp(mx - smax)
                    kth = mx
                else:
                    thr = 0.5 * (kth + mx)
                work = jnp.where(work == mx, neg_inf, work)
            half_inv_z = 0.5 / zsum
            e1_ref[hh] = jnp.exp(sc_ref[0] - top_ref[0:1, :]) * half_inv_z
            e2 = jnp.exp(sc_ref[1] - top_ref[topk:topk + 1, :])
            e2_ref[hh] = e2.astype(BF16).reshape(n_keys // PACK, PACK, tt)
            thr_ref[hh] = jnp.broadcast_to(jnp.exp(thr - smax) * half_inv_z, (PACK, tt)).astype(BF16)

    n_il = ec // n_keys
    i0 = pl.multiple_of(e * n_il, n_il)
    e1_blk = [e1_ref[hh, pl.ds(i0, n_il), :] for hh in range(n_heads)]
    ht = ht_ref[...]

    def activation(il):
        return _dot(u_ref[il * n_keys:(il + 1) * n_keys, :], ht)

    d_half = yt_ref.shape[0] // 2

    def output_update(il0, part):
        rows = slice(il0 * n_keys, (il0 + 2) * n_keys)
        drows = slice(part * d_half, (part + 1) * d_half)
        yt_ref[drows, :] += _dot(vt_ref[drows, rows], jnp.concatenate(pws[il0:il0 + 2], axis=0))

    act_next = activation(0)
    pws = []
    for il in range(n_il):
        act = act_next
        if il + 1 < n_il:
            act_next = activation(il + 1)
        if il >= 2:
            output_update(il - 2 - il % 2, il % 2)
        wsum = jnp.zeros((n_keys // PACK, PACK, tt), BF16)
        for hh in range(n_heads):
            e1_row = jnp.broadcast_to(e1_blk[hh][il:il + 1, :], (PACK, tt)).astype(BF16)
            gate = e1_row[None] * e2_ref[hh]
            wsum = wsum + jnp.where(gate >= thr_ref[hh][None], gate, jnp.zeros_like(gate))
        pw = _two_gelu(act).astype(BF16).reshape(n_keys // PACK, PACK, tt) * wsum
        pws.append(pw.reshape(n_keys, tt))
    output_update(n_il - 2, 0)
    output_update(n_il - 2, 1)

    @pl.when(e == n_chunks - 1)
    def _():
        o_ref[...] = x_ref[...] + yt_ref[...].T


def _stage_peer(x1, p, *, tt, ec):
    n, d = x1.shape
    hp, nk, topk = p["n_peer_heads"], p["n_keys"], PEER_TOPK
    nc = p["u"].shape[0] // ec
    cand_pairs = tuple((a, c) for a in range(topk) for c in range(topk) if (a + 1) * (c + 1) <= topk)
    n_cand_rows = -(-len(cand_pairs) // 8) * 8
    body = functools.partial(_peer_body, tt=tt, n_heads=hp, n_keys=nk, topk=topk, ec=ec, cand_pairs=cand_pairs)
    return pl.pallas_call(
        body, grid=(n // tt, nc),
        in_specs=[
            pl.BlockSpec((tt, d), lambda ti, ei: (ti, 0)),
            _full(p["n2g"].shape), _full(p["wpqt"].shape), _full(p["keys"].shape),
            pl.BlockSpec((ec, d), lambda ti, ei: (ei, 0)),
            pl.BlockSpec((d, ec), lambda ti, ei: (0, ei)),
        ],
        out_specs=pl.BlockSpec((tt, d), lambda ti, ei: (ti, 0)),
        out_shape=jax.ShapeDtypeStruct((n, d), F32),
        scratch_shapes=[
            pltpu.VMEM((d, tt), BF16),
            pltpu.VMEM((d, tt), F32),
            pltpu.VMEM((2, nk, tt), F32),
            pltpu.VMEM((hp, nk // PACK, PACK, tt), BF16),
            pltpu.VMEM((hp, nk, tt), F32),
            pltpu.VMEM((hp, PACK, tt), BF16),
            pltpu.VMEM((2 * topk, tt), F32),
            pltpu.VMEM((n_cand_rows, tt), F32),
        ],
        compiler_params=pltpu.CompilerParams(
            dimension_semantics=("arbitrary", "arbitrary"), vmem_limit_bytes=VMEM_LIMIT),
        name="peer",
    )(x1, p["n2g"], p["wpqt"], p["keys"], p["u"], p["vt"])


def _prepare(norm1_g, w_in, conv_w, a_log, dt_bias, gdn_norm_g, w_gdn_out, cq_norm_g, w_uq, ckv_norm_g,
             w_ukv, q_norm_g, k_norm_g, w_mla_out, w_o, norm2_g, w_pq, sub_keys, u_tab, v_tab):
    d = w_in.shape[1]
    hg = a_log.shape[-1]
    dv = gdn_norm_g.shape[-1]
    v_w = w_gdn_out.shape[1]
    qkv_w = conv_w.shape[-1]
    qk_w = (qkv_w - v_w) // 2
    dk = qk_w // hg
    q_lora = cq_norm_g.shape[-1]
    kv_lora = ckv_norm_g.shape[-1]
    mla_qk = q_norm_g.shape[-1]
    hm = w_uq.shape[-1] // mla_qk
    mla_v = w_mla_out.shape[1] // hm
    nope = w_ukv.shape[-1] // hm - mla_v
    rope = mla_qk - nope
    assert mla_qk <= LANES and 2 * hg <= LANES and dk == dv

    win = w_in[0]
    o = 0
    cols = {}
    for name, width in (("qkv", qkv_w), ("z", v_w), ("b", hg), ("a", hg), ("cq", q_lora), ("ckv", kv_lora),
                        ("krot", rope), ("gates", 2 * d)):
        cols[name] = win[:, o:o + width]
        o += width
    assert o == win.shape[1]

    p = dict(n_gdn_heads=hg, dk=dk, dv=dv, qk_w=qk_w, v_w=v_w, n_mla_heads=hm, mla_qk=mla_qk, mla_v=mla_v,
             nope=nope, rope=rope, conv_k=conv_w.shape[1])
    p["n1g"] = norm1_g[0][None, :]
    p["wqkv"] = cols["qkv"].astype(BF16)
    p["convw"] = conv_w[0]
    head_of = np.arange(qk_w) // dk
    p["seg"] = jnp.asarray((head_of[:, None] == head_of[None, :]).astype(np.float32)).astype(BF16)
    p["wz"] = cols["z"].astype(BF16)
    wbg = jnp.zeros((d, LANES), F32).at[:, :hg].set(cols["b"]).at[:, hg:2 * hg].set(cols["a"])
    p["wbg"] = wbg.astype(BF16)
    nexp_a = jnp.zeros((LANES,), F32).at[hg:2 * hg].set(-jnp.exp(a_log[0]))
    dtb = jnp.zeros((LANES,), F32).at[hg:2 * hg].set(dt_bias[0])
    p["bgpar"] = jnp.stack([nexp_a, dtb])
    p["wcq"] = cols["cq"].astype(BF16)
    p["cqg"] = cq_norm_g[0][None, :]
    wuq = w_uq[0].reshape(q_lora, hm, mla_qk)
    p["wuq"] = jnp.pad(wuq, ((0, 0), (0, 0), (0, LANES - mla_qk))).reshape(q_lora, hm * LANES).astype(BF16)
    p["qng"] = jnp.pad(q_norm_g[0] * (mla_qk ** -0.5 * np.log2(np.e)), (0, LANES - mla_qk))[None, :]
    p["wckv"] = cols["ckv"].astype(BF16)
    p["ckvg"] = ckv_norm_g[0][None, :]
    wukv = w_ukv[0].reshape(kv_lora, hm, nope + mla_v)
    p["wuk"] = jnp.pad(wukv[:, :, :nope], ((0, 0), (0, 0), (0, LANES - nope))).reshape(kv_lora, hm * LANES).astype(BF16)
    p["wuvt"] = wukv[:, :, nope:].reshape(kv_lora, hm * mla_v).T.astype(BF16)
    p["wkrot"] = jnp.zeros((d, LANES), F32).at[:, nope:nope + rope].set(cols["krot"]).astype(BF16)
    p["kng"] = jnp.pad(k_norm_g[0], (0, LANES - mla_qk))[None, :]
    half = rope // 2
    inv = np.power(ROPE_THETA, -np.arange(half, dtype=np.float32) / half).astype(np.float32)
    invf = np.zeros((LANES,), np.float32)
    invf[nope:nope + half] = inv
    invf[nope + half:nope + rope] = inv
    sign = np.zeros((LANES,), np.float32)
    sign[nope:nope + half] = -1.0
    sign[nope + half:nope + rope] = 1.0
    p["ropepar"] = jnp.asarray(np.stack([invf, sign]))
    p["wg"] = cols["gates"].astype(BF16)

    p["gng"] = gdn_norm_g[0][None, :]
    p["wgo"] = w_gdn_out[0].astype(BF16)
    p["wmo"] = w_mla_out[0].astype(BF16)
    p["wo"] = w_o[0].astype(BF16)

    hp, _, nk, dkey = sub_keys.shape[1:]
    p["n_peer_heads"] = hp
    p["n_keys"] = nk
    assert nk == LANES and dkey == LANES
    p["n2g"] = norm2_g[0][None, :]
    p["wpqt"] = w_pq[0].T.astype(BF16)
    p["keys"] = sub_keys[0].reshape(hp * 2 * nk, dkey).astype(BF16)
    p["u"] = u_tab[0].astype(BF16)
    p["vt"] = v_tab[0].T.astype(BF16)
    return p


def _block_sizes(b, s):
    return dict(
        inproj_tokens=min(256, s),
        gdn_tokens=min(CHUNK * GDN_GROUP, s),
        attn_block=min(512, s),
        merge_tokens=min(512, s),
        peer_tokens=min(512, b * s),
        peer_experts=2048,
    )


def kernel(x, positions, norm1_g, w_in, conv_w, a_log, dt_bias, gdn_norm_g, w_gdn_out, cq_norm_g, w_uq,
           ckv_norm_g, w_ukv, q_norm_g, k_norm_g, w_mla_out, w_o, norm2_g, w_pq, sub_keys, u_tab, v_tab):
    assert w_in.shape[0] == 1, "single-layer block"
    b, s, d = x.shape
    p = _prepare(norm1_g, w_in, conv_w, a_log, dt_bias, gdn_norm_g, w_gdn_out, cq_norm_g, w_uq, ckv_norm_g,
                 w_ukv, q_norm_g, k_norm_g, w_mla_out, w_o, norm2_g, w_pq, sub_keys, u_tab, v_tab)
    blk = _block_sizes(b, s)
    gq, gk, gv, sz, bgc, sg, qp, kp, vt = _stage_inproj(x, positions, p, tm=blk["inproj_tokens"])
    oa = _stage_gdn(gq, gk, gv, bgc, sz, p, b=b, s=s, ct=blk["gdn_tokens"])
    att = _stage_attn(qp, kp, vt, blk=blk["attn_block"])
    x1 = _stage_merge(x.reshape(b * s, d), oa, att.reshape(b * s, -1), sg, p, tm=blk["merge_tokens"])
    out = _stage_peer(x1, p, tt=blk["peer_tokens"], ec=blk["peer_experts"])
    return out.reshape(b, s, d)
```

```python
import functools

import jax
import jax.numpy as jnp
import numpy as np
from jax import lax
from jax.experimental import pallas as pl
from jax.experimental.pallas import tpu as pltpu

F32 = jnp.float32
BF16 = jnp.bfloat16

CHUNK = 64
NORM_EPS = 1e-6
ROPE_THETA = 10000.0
PEER_TOPK = 16
LANES = 128
PACK = 16
GDN_GROUP = 4
VMEM_LIMIT = 56 * 1024 * 1024

NT_DIMS = (((1,), (1,)), ((), ()))
TN_DIMS = (((0,), (0,)), ((), ()))


def _dot(a, b):
    return jnp.dot(a, b, preferred_element_type=F32)


def _dot_nt(a, b):
    return lax.dot_general(a, b, NT_DIMS, preferred_element_type=F32)


def _dot_tn(a, b):
    return lax.dot_general(a, b, TN_DIMS, preferred_element_type=F32)


def _rms(v, g):
    ms = jnp.mean(v * v, axis=-1, keepdims=True)
    return v * lax.rsqrt(ms + NORM_EPS) * g


def _sigmoid(v):
    return 1.0 / (1.0 + jnp.exp(-v))


def _softplus(v):
    return jnp.maximum(v, 0.0) + jnp.log(1.0 + jnp.exp(-jnp.abs(v)))


def _full(shape):
    n = len(shape)
    return pl.BlockSpec(shape, lambda *_: (0,) * n)


def _inproj_body(x_ref, xprev_ref, pos_ref, n1g_ref, wqkv_ref, convw_ref, seg_ref, wz_ref,
                 wbg_ref, bgpar_ref, wcq_ref, cqg_ref, wuq_ref, qng_ref, wckv_ref, ckvg_ref,
                 wuk_ref, wuvt_ref, wkrot_ref, kng_ref, ropepar_ref, wg_ref,
                 gq_ref, gk_ref, gv_ref, sz_ref, bgc_ref, sg_ref, qp_ref, kp_ref, vt_ref,
                 *, tm, qk_w, dk, n_heads, d_rot_lo, d_rot_half, d_qk, dv_mla, conv_k):
    s = pl.program_id(1)
    n1g = n1g_ref[...]
    h = _rms(x_ref[0], n1g).astype(BF16)
    hp = _rms(xprev_ref[0, 0], n1g).astype(BF16)

    pre = _dot(h, wqkv_ref[...])
    pre_prev = _dot(hp, wqkv_ref[...])
    pre_prev = jnp.where(s == 0, 0.0, pre_prev)
    ext = jnp.concatenate([pre_prev, pre], axis=0)
    cw = convw_ref[...]
    conv = cw[0:1, :] * ext[8 - conv_k + 1:8 - conv_k + 1 + tm, :]
    for k in range(1, conv_k):
        off = 8 - conv_k + 1 + k
        conv = conv + cw[k:k + 1, :] * ext[off:off + tm, :]
    act = conv * _sigmoid(conv)
    seg = seg_ref[...]

    def l2n(t):
        ss = _dot((t * t).astype(BF16), seg)
        return t * lax.rsqrt(ss + NORM_EPS)

    gq_ref[...] = (l2n(act[:, :qk_w]) * (dk ** -0.5)).astype(BF16)
    gk_ref[...] = l2n(act[:, qk_w:2 * qk_w]).astype(BF16)
    gv_ref[...] = act[:, 2 * qk_w:].astype(BF16)

    z = _dot(h, wz_ref[...])
    sz_ref[...] = (z * _sigmoid(z)).astype(BF16)

    raw = _dot(h, wbg_ref[...])
    lane = lax.broadcasted_iota(jnp.int32, raw.shape, 1)
    g = bgpar_ref[0:1, :] * _softplus(raw + bgpar_ref[1:2, :])
    bgc_ref[...] = jnp.where(lane < n_heads, _sigmoid(raw), g)

    pos = pos_ref[0].astype(F32)
    ang = pos * ropepar_ref[0:1, :]
    cosv = jnp.cos(ang)
    sinv = jnp.sin(ang) * ropepar_ref[1:2, :]
    lane128 = lax.broadcasted_iota(jnp.int32, ang.shape, 1)
    first_half = lane128 < d_rot_lo + d_rot_half

    def rope(y):
        partner = jnp.where(first_half, pltpu.roll(y, LANES - d_rot_half, 1), pltpu.roll(y, d_rot_half, 1))
        return y * cosv + partner * sinv

    def head_norm(slab, gain):
        ms = jnp.sum(slab * slab, axis=-1, keepdims=True) * (1.0 / d_qk)
        return slab * lax.rsqrt(ms + NORM_EPS) * gain

    cqn = _rms(_dot(h, wcq_ref[...]), cqg_ref[...]).astype(BF16)
    qall = _dot(cqn, wuq_ref[...])
    ckvn = _rms(_dot(h, wckv_ref[...]), ckvg_ref[...]).astype(BF16)
    kall = _dot(ckvn, wuk_ref[...])
    krot = _dot(h, wkrot_ref[...])
    qng = qng_ref[...]
    kng = kng_ref[...]
    for hh in range(n_heads):
        sl = slice(hh * LANES, (hh + 1) * LANES)
        qp_ref[0, hh] = rope(head_norm(qall[:, sl], qng)).astype(BF16)
        kp_ref[0, hh] = rope(head_norm(kall[:, sl] + krot, kng)).astype(BF16)
    vt = _dot_nt(wuvt_ref[...], ckvn)
    for hh in range(n_heads):
        vt_ref[0, hh] = vt[hh * dv_mla:(hh + 1) * dv_mla, :].astype(BF16)

    sg_ref[...] = _sigmoid(_dot(h, wg_ref[...])).astype(BF16)


def _stage_inproj(x, positions, p, *, tm):
    b, s, d = x.shape
    n = b * s
    nt = s // tm
    hm = p["n_mla_heads"]
    qk_w = p["qk_w"]
    body = functools.partial(
        _inproj_body, tm=tm, qk_w=qk_w, dk=p["dk"], n_heads=p["n_gdn_heads"],
        d_rot_lo=p["nope"], d_rot_half=p["rope"] // 2, d_qk=p["mla_qk"], dv_mla=p["mla_v"],
        conv_k=p["conv_k"])
    weights = [p["n1g"], p["wqkv"], p["convw"], p["seg"], p["wz"], p["wbg"], p["bgpar"], p["wcq"],
               p["cqg"], p["wuq"], p["qng"], p["wckv"], p["ckvg"], p["wuk"], p["wuvt"], p["wkrot"],
               p["kng"], p["ropepar"], p["wg"]]
    flat = lambda w: pl.BlockSpec((tm, w), lambda bi, si: (bi * nt + si, 0))
    in_specs = [
        pl.BlockSpec((1, tm, d), lambda bi, si: (bi, si, 0)),
        pl.BlockSpec((1, 1, 8, d), lambda bi, si: (bi, jnp.maximum(si * (tm // 8) - 1, 0), 0, 0)),
        pl.BlockSpec((1, tm, 1), lambda bi, si: (bi, si, 0)),
    ] + [_full(w.shape) for w in weights]
    v_w = p["v_w"]
    out_shape = [
        jax.ShapeDtypeStruct((n, qk_w), BF16), jax.ShapeDtypeStruct((n, qk_w), BF16),
        jax.ShapeDtypeStruct((n, v_w), BF16), jax.ShapeDtypeStruct((n, v_w), BF16),
        jax.ShapeDtypeStruct((n, LANES), F32), jax.ShapeDtypeStruct((n, 2 * d), BF16),
        jax.ShapeDtypeStruct((b, hm, s, LANES), BF16), jax.ShapeDtypeStruct((b, hm, s, LANES), BF16),
        jax.ShapeDtypeStruct((b, hm, p["mla_v"], s), BF16),
    ]
    out_specs = [
        flat(qk_w), flat(qk_w), flat(v_w), flat(v_w), flat(LANES), flat(2 * d),
        pl.BlockSpec((1, hm, tm, LANES), lambda bi, si: (bi, 0, si, 0)),
        pl.BlockSpec((1, hm, tm, LANES), lambda bi, si: (bi, 0, si, 0)),
        pl.BlockSpec((1, hm, p["mla_v"], tm), lambda bi, si: (bi, 0, 0, si)),
    ]
    return pl.pallas_call(
        body, grid=(b, nt), in_specs=in_specs, out_specs=out_specs, out_shape=out_shape,
        compiler_params=pltpu.CompilerParams(
            dimension_semantics=("arbitrary", "arbitrary"), vmem_limit_bytes=VMEM_LIMIT),
        name="inproj",
    )(x, x.reshape(b, s // 8, 8, d), positions.reshape(b, s, 1), *weights)


def _gdn_body(q_ref, k_ref, v_ref, bg_ref, sz_ref, gng_ref, o_ref, state_ref, *, n_chunks, n_heads, dk, dv):
    c = CHUNK

    @pl.when(pl.program_id(1) == 0)
    def _():
        state_ref[...] = jnp.zeros_like(state_ref)

    r_io = lax.broadcasted_iota(jnp.int32, (c, c), 0)
    c_io = lax.broadcasted_iota(jnp.int32, (c, c), 1)
    causal = r_io >= c_io
    strict = r_io > c_io
    tri_incl = causal.astype(F32)
    tri_upper = (r_io <= c_io).astype(F32)
    eye = (r_io == c_io).astype(F32)
    same16 = (r_io // 16) == (c_io // 16)
    same32 = (r_io // 32) == (c_io // 32)
    gng = gng_ref[...]

    def mm(a, b):
        return _dot(a.astype(BF16), b.astype(BF16))

    def group_step(gi, carry):
        items = [(cc, hh) for cc in range(GDN_GROUP) for hh in range(n_heads)]
        rows = [pl.ds(pl.multiple_of((gi * GDN_GROUP + cc) * c, c), c) for cc in range(GDN_GROUP)]
        bgs = [bg_ref[r, :] for r in rows]
        gc_cols = [jnp.dot(tri_incl, bg, preferred_element_type=F32, precision=lax.Precision.HIGHEST)
                   for bg in bgs]
        gc_rows = [lax.dot_general(bg, tri_upper, TN_DIMS, preferred_element_type=F32,
                                   precision=lax.Precision.HIGHEST) for bg in bgs]
        q_all = [q_ref[r, :].astype(F32) for r in rows]
        k_all = [k_ref[r, :].astype(F32) for r in rows]
        v_all = [v_ref[r, :].astype(F32) for r in rows]
        sz_all = [sz_ref[r, :].astype(F32) for r in rows]
        states = [state_ref[hh] for hh in range(n_heads)]
        qs = [q_all[cc][:, hh * dk:(hh + 1) * dk] for cc, hh in items]
        ks = [k_all[cc][:, hh * dk:(hh + 1) * dk] for cc, hh in items]
        vs = [v_all[cc][:, hh * dv:(hh + 1) * dv] for cc, hh in items]
        betas = [bgs[cc][:, hh:hh + 1] for cc, hh in items]
        gccs = [gc_cols[cc][:, n_heads + hh:n_heads + hh + 1] for cc, hh in items]
        gcrs = [gc_rows[cc][n_heads + hh:n_heads + hh + 1, :] for cc, hh in items]
        n_it = range(len(items))
        glasts = [g[c - 1:c, :] for g in gccs]
        decs = [jnp.where(causal, jnp.exp(jnp.minimum(gccs[t] - gcrs[t], 0.0)), 0.0) for t in n_it]
        kbs = [ks[t] * betas[t] for t in n_it]
        egs = [jnp.exp(g) for g in gccs]
        akqs = [_dot_nt(jnp.concatenate([kbs[t], qs[t]], axis=0).astype(BF16), ks[t].astype(BF16))
                for t in n_it]
        lowers = [jnp.where(strict, akqs[t][:c] * decs[t], 0.0) for t in n_it]
        attns = [akqs[t][c:] * decs[t] for t in n_it]
        npows = [jnp.where(same16, -lo, 0.0) for lo in lowers]
        ts = [eye + nd for nd in npows]
        for _ in range(3):
            npows = [mm(a, a) for a in npows]
            ts = [ts[t] + mm(ts[t], npows[t]) for t in n_it]
        off32 = jnp.logical_and(same32, jnp.logical_not(same16))
        tb = [mm(ts[t], jnp.where(off32, lowers[t], 0.0)) for t in n_it]
        ts = [ts[t] - mm(tb[t], ts[t]) for t in n_it]
        tb = [mm(ts[t], jnp.where(same32, 0.0, lowers[t])) for t in n_it]
        ts = [ts[t] - mm(tb[t], ts[t]) for t in n_it]
        uws = [mm(ts[t], jnp.concatenate([vs[t] * betas[t], kbs[t] * egs[t]], axis=1))
               for t in n_it]
        wqgs = [jnp.concatenate([uws[t][:, dv:], qs[t] * egs[t]], axis=0) for t in n_it]
        kdecs = [(ks[t] * jnp.exp(glasts[t] - gccs[t])).astype(BF16) for t in n_it]
        for cc in range(GDN_GROUP):
            base = cc * n_heads
            heads = range(n_heads)
            wqss = [mm(wqgs[base + hh], states[hh]) for hh in heads]
            v_news = [uws[base + hh][:, :dv] - wqss[hh][:c] for hh in heads]
            os_ = [wqss[hh][c:] + mm(attns[base + hh], v_news[hh]) for hh in heads]
            states = [states[hh] * jnp.exp(glasts[base + hh]) + _dot_tn(kdecs[base + hh], v_news[hh].astype(BF16))
                      for hh in heads]
            outs = [o * lax.rsqrt(jnp.mean(o * o, axis=-1, keepdims=True) + NORM_EPS) * gng for o in os_]
            o_ref[rows[cc], :] = (jnp.concatenate(outs, axis=1) * sz_all[cc]).astype(BF16)
        for hh in range(n_heads):
            state_ref[hh] = states[hh]
        return carry

    lax.fori_loop(0, n_chunks // GDN_GROUP, group_step, 0)


def _stage_gdn(gq, gk, gv, bgc, sz, p, *, b, s, ct):
    n = b * s
    nt = s // ct
    hg, dk, dv = p["n_gdn_heads"], p["dk"], p["dv"]
    body = functools.partial(_gdn_body, n_chunks=ct // CHUNK, n_heads=hg, dk=dk, dv=dv)
    flat = lambda w: pl.BlockSpec((ct, w), lambda bi, si: (bi * nt + si, 0))
    return pl.pallas_call(
        body, grid=(b, nt),
        in_specs=[flat(hg * dk), flat(hg * dk), flat(hg * dv), flat(LANES), flat(hg * dv), _full(p["gng"].shape)],
        out_specs=flat(hg * dv),
        out_shape=jax.ShapeDtypeStruct((n, hg * dv), BF16),
        scratch_shapes=[pltpu.VMEM((hg, dk, dv), F32)],
        compiler_params=pltpu.CompilerParams(
            dimension_semantics=("arbitrary", "arbitrary"), vmem_limit_bytes=VMEM_LIMIT),
        name="gdn",
    )(gq, gk, gv, bgc, sz, p["gng"])


def _attn_body(itab_ref, jtab_ref, q_ref, k_ref, vt_ref, o_ref, acc_ref, m_ref, l_ref, *, n_heads, dv, bq, bk):
    pidx = pl.program_id(1)
    i = itab_ref[pidx]
    j = jtab_ref[pidx]

    @pl.when(j == 0)
    def _():
        acc_ref[...] = jnp.zeros_like(acc_ref)
        m_ref[...] = jnp.full_like(m_ref, -jnp.inf)
        l_ref[...] = jnp.zeros_like(l_ref)

    def step(masked):
        if masked:
            kc = (j * bk + lax.broadcasted_iota(jnp.int32, (bk, bq), 0)) // CHUNK
            qc = (i * bq + lax.broadcasted_iota(jnp.int32, (bk, bq), 1)) // CHUNK
            allowed = kc <= qc
        def scores(hh):
            return _dot_nt(k_ref[0, hh], q_ref[0, hh])

        def col_max(st, hh):
            if masked:
                st = jnp.where(allowed, st, -jnp.inf)
            return st, jnp.maximum(m_ref[hh:hh + 1, :], jnp.max(st, axis=0, keepdims=True))

        st_cur, m_cur = col_max(scores(0), 0)
        st_next = scores(1)
        for hh in range(n_heads):
            st, m_new = st_cur, m_cur
            if hh + 2 < n_heads:
                st_after = scores(hh + 2)
            if hh + 1 < n_heads:
                st_cur, m_cur = col_max(st_next, hh + 1)
                st_next = st_after if hh + 2 < n_heads else None
            m_prev = m_ref[hh:hh + 1, :]
            alpha = jnp.exp2(m_prev - m_new)
            pt = jnp.exp2(st - m_new)
            l_ref[hh:hh + 1, :] = alpha * l_ref[hh:hh + 1, :] + jnp.sum(pt, axis=0, keepdims=True)
            m_ref[hh:hh + 1, :] = m_new
            rows = slice(hh * dv, (hh + 1) * dv)
            acc_ref[rows, :] = acc_ref[rows, :] * alpha + _dot(vt_ref[0, hh], pt.astype(BF16))

    @pl.when(j < i)
    def _():
        step(False)

    @pl.when(j == i)
    def _():
        step(True)
        inv = 1.0 / l_ref[...]
        for hh in range(n_heads):
            rows = slice(hh * dv, (hh + 1) * dv)
            acc_ref[rows, :] = acc_ref[rows, :] * inv[hh:hh + 1, :]
        o_ref[0] = acc_ref[...].T.astype(BF16)


def _stage_attn(qp, kp, vt, *, blk):
    b, hm, s, _ = qp.shape
    dv = vt.shape[2]
    nb = s // blk
    pairs = [(i, j) for i in range(nb) for j in range(i + 1)]
    itab = jnp.asarray(np.array([pq[0] for pq in pairs], np.int32))
    jtab = jnp.asarray(np.array([pq[1] for pq in pairs], np.int32))
    body = functools.partial(_attn_body, n_heads=hm, dv=dv, bq=blk, bk=blk)
    grid_spec = pltpu.PrefetchScalarGridSpec(
        num_scalar_prefetch=2, grid=(b, len(pairs)),
        in_specs=[
            pl.BlockSpec((1, hm, blk, LANES), lambda bi, pi, it, jt: (bi, 0, it[pi], 0)),
            pl.BlockSpec((1, hm, blk, LANES), lambda bi, pi, it, jt: (bi, 0, jt[pi], 0)),
            pl.BlockSpec((1, hm, dv, blk), lambda bi, pi, it, jt: (bi, 0, 0, jt[pi])),
        ],
        out_specs=pl.BlockSpec((1, blk, hm * dv), lambda bi, pi, it, jt: (bi, it[pi], 0)),
        scratch_shapes=[pltpu.VMEM((hm * dv, blk), F32), pltpu.VMEM((hm, blk), F32), pltpu.VMEM((hm, blk), F32)],
    )
    return pl.pallas_call(
        body, grid_spec=grid_spec,
        out_shape=jax.ShapeDtypeStruct((b, s, hm * dv), BF16),
        compiler_params=pltpu.CompilerParams(
            dimension_semantics=("arbitrary", "arbitrary"), vmem_limit_bytes=VMEM_LIMIT),
        name="attn",
    )(itab, jtab, qp, kp, vt)


def _merge_body(x_ref, oa_ref, att_ref, sg_ref, wgo_ref, wmo_ref, wo_ref, o_ref, *, d):
    ya = _dot(oa_ref[...], wgo_ref[...])
    yb = _dot(att_ref[...], wmo_ref[...])
    merged = sg_ref[:, :d].astype(F32) * ya + sg_ref[:, d:].astype(F32) * yb
    o_ref[...] = x_ref[...] + _dot(merged.astype(BF16), wo_ref[...])


def _stage_merge(x2, oa, att, sg, p, *, tm):
    n, d = x2.shape
    row = lambda w: pl.BlockSpec((tm, w), lambda i: (i, 0))
    return pl.pallas_call(
        functools.partial(_merge_body, d=d), grid=(n // tm,),
        in_specs=[row(d), row(oa.shape[1]), row(att.shape[1]), row(2 * d),
                  _full(p["wgo"].shape), _full(p["wmo"].shape), _full(p["wo"].shape)],
        out_specs=row(d), out_shape=jax.ShapeDtypeStruct((n, d), F32),
        compiler_params=pltpu.CompilerParams(dimension_semantics=("arbitrary",), vmem_limit_bytes=VMEM_LIMIT),
        name="merge",
    )(x2, oa, att, sg, p["wgo"], p["wmo"], p["wo"])


def _sort_desc(vals):
    n = len(vals)
    vals = list(vals)
    p = 1
    while p < n:
        k = p
        while k >= 1:
            for j in range(k % p, n - k, 2 * k):
                for i in range(min(k, n - j - k)):
                    if (i + j) // (2 * p) == (i + j + k) // (2 * p):
                        a, b = vals[i + j], vals[i + j + k]
                        vals[i + j], vals[i + j + k] = jnp.maximum(a, b), jnp.minimum(a, b)
            k //= 2
        p *= 2
    return vals


def _two_gelu(v):
    return v * (1.0 + lax.erf(v * (2.0 ** -0.5)))


def _peer_body(x_ref, n2g_ref, wpqt_ref, keys_ref, u_ref, vt_ref, o_ref,
               ht_ref, yt_ref, sc_ref, e2_ref, e1_ref, thr_ref, top_ref, cand_ref,
               *, tt, n_heads, n_keys, topk, ec, cand_pairs):
    e = pl.program_id(1)
    n_chunks = pl.num_programs(1)
    neg_inf = -jnp.inf

    @pl.when(e == 0)
    def _():
        h2 = _rms(x_ref[...], n2g_ref[...])
        ht = h2.T.astype(BF16)
        ht_ref[...] = ht
        yt_ref[...] = jnp.zeros_like(yt_ref)
        qt = _dot(wpqt_ref[...], ht).astype(BF16)
        for hh in range(n_heads):
            for half in range(2):
                r0 = (hh * 2 + half) * n_keys
                sc = _dot(keys_ref[r0:r0 + n_keys, :], qt[r0:r0 + n_keys, :])
                sc_ref[half] = sc
                cols = _sort_desc([sc[8 * g:8 * g + 8, :] for g in range(n_keys // 8)])[:topk]
                for r in range(topk):
                    mx = jnp.max(cols[0], axis=0, keepdims=True)
                    top_ref[half * topk + r:half * topk + r + 1, :] = mx
                    hit = cols[0] == mx
                    cols = [jnp.where(hit, cols[k + 1], cols[k]) for k in range(topk - r - 1)]
            for ci, (r1, r2) in enumerate(cand_pairs):
                cand_ref[ci:ci + 1, :] = top_ref[r1:r1 + 1, :] + top_ref[topk + r2:topk + r2 + 1, :]
            n_cand = len(cand_pairs)
            pad = cand_ref.shape[0] - n_cand
            if pad:
                cand_ref[n_cand:, :] = jnp.full((pad, tt), neg_inf, F32)
            work = cand_ref[...]
            smax = top_ref[0:1, :] + top_ref[topk:topk + 1, :]
            zsum = jnp.zeros((1, tt), F32)
            kth = smax
            for r in range(topk + 1):
                mx = jnp.max(work, axis=0, keepdims=True)
                if r < topk:
                    zsum = zsum + jnp.exp(mx - smax)
                    kth = mx
                else:
                    thr = 0.5 * (kth + mx)
                work = jnp.where(work == mx, neg_inf, work)
            half_inv_z = 0.5 / zsum
            e1_ref[hh] = jnp.exp(sc_ref[0] - top_ref[0:1, :]) * half_inv_z
            e2 = jnp.exp(sc_ref[1] - top_ref[topk:topk + 1, :])
            e2_ref[hh] = e2.astype(BF16).reshape(n_keys // PACK, PACK, tt)
            thr_ref[hh] = jnp.broadcast_to(jnp.exp(thr - smax) * half_inv_z, (PACK, tt)).astype(BF16)

    n_il = ec // n_keys
    i0 = pl.multiple_of(e * n_il, n_il)
    e1_blk = [e1_ref[hh, pl.ds(i0, n_il), :] for hh in range(n_heads)]
    ht = ht_ref[...]

    def activation(il):
        return _dot(u_ref[il * n_keys:(il + 1) * n_keys, :], ht)

    def output_update(il0):
        rows = slice(il0 * n_keys, (il0 + 2) * n_keys)
        yt_ref[...] += _dot(vt_ref[:, rows], jnp.concatenate(pws[il0:il0 + 2], axis=0))

    act_next = activation(0)
    pws = []
    for il in range(n_il):
        act = act_next
        if il + 1 < n_il:
            act_next = activation(il + 1)
        if il % 2 == 0 and il >= 2:
            output_update(il - 2)
        wsum = jnp.zeros((n_keys // PACK, PACK, tt), BF16)
        for hh in range(n_heads):
            e1_row = jnp.broadcast_to(e1_blk[hh][il:il + 1, :], (PACK, tt)).astype(BF16)
            gate = e1_row[None] * e2_ref[hh]
            wsum = wsum + jnp.where(gate >= thr_ref[hh][None], gate, jnp.zeros_like(gate))
        pw = _two_gelu(act).astype(BF16).reshape(n_keys // PACK, PACK, tt) * wsum
        pws.append(pw.reshape(n_keys, tt))
    output_update(n_il - 2)

    @pl.when(e == n_chunks - 1)
    def _():
        o_ref[...] = x_ref[...] + yt_ref[...].T


def _stage_peer(x1, p, *, tt, ec):
    n, d = x1.shape
    hp, nk, topk = p["n_peer_heads"], p["n_keys"], PEER_TOPK
    nc = p["u"].shape[0] // ec
    cand_pairs = tuple((a, c) for a in range(topk) for c in range(topk) if (a + 1) * (c + 1) <= topk)
    n_cand_rows = -(-len(cand_pairs) // 8) * 8
    body = functools.partial(_peer_body, tt=tt, n_heads=hp, n_keys=nk, topk=topk, ec=ec, cand_pairs=cand_pairs)
    return pl.pallas_call(
        body, grid=(n // tt, nc),
        in_specs=[
            pl.BlockSpec((tt, d), lambda ti, ei: (ti, 0)),
            _full(p["n2g"].shape), _full(p["wpqt"].shape), _full(p["keys"].shape),
            pl.BlockSpec((ec, d), lambda ti, ei: (ei, 0)),
            pl.BlockSpec((d, ec), lambda ti, ei: (0, ei)),
        ],
        out_specs=pl.BlockSpec((tt, d), lambda ti, ei: (ti, 0)),
        out_shape=jax.ShapeDtypeStruct((n, d), F32),
        scratch_shapes=[
            pltpu.VMEM((d, tt), BF16),
            pltpu.VMEM((d, tt), F32),
            pltpu.VMEM((2, nk, tt), F32),
            pltpu.VMEM((hp, nk // PACK, PACK, tt), BF16),
            pltpu.VMEM((hp, nk, tt), F32),
            pltpu.VMEM((hp, PACK, tt), BF16),
            pltpu.VMEM((2 * topk, tt), F32),
            pltpu.VMEM((n_cand_rows, tt), F32),
        ],
        compiler_params=pltpu.CompilerParams(
            dimension_semantics=("arbitrary", "arbitrary"), vmem_limit_bytes=VMEM_LIMIT),
        name="peer",
    )(x1, p["n2g"], p["wpqt"], p["keys"], p["u"], p["vt"])


def _prepare(norm1_g, w_in, conv_w, a_log, dt_bias, gdn_norm_g, w_gdn_out, cq_norm_g, w_uq, ckv_norm_g,
             w_ukv, q_norm_g, k_norm_g, w_mla_out, w_o, norm2_g, w_pq, sub_keys, u_tab, v_tab):
    d = w_in.shape[1]
    hg = a_log.shape[-1]
    dv = gdn_norm_g.shape[-1]
    v_w = w_gdn_out.shape[1]
    qkv_w = conv_w.shape[-1]
    qk_w = (qkv_w - v_w) // 2
    dk = qk_w // hg
    q_lora = cq_norm_g.shape[-1]
    kv_lora = ckv_norm_g.shape[-1]
    mla_qk = q_norm_g.shape[-1]
    hm = w_uq.shape[-1] // mla_qk
    mla_v = w_mla_out.shape[1] // hm
    nope = w_ukv.shape[-1] // hm - mla_v
    rope = mla_qk - nope
    assert mla_qk <= LANES and 2 * hg <= LANES and dk == dv

    win = w_in[0]
    o = 0
    cols = {}
    for name, width in (("qkv", qkv_w), ("z", v_w), ("b", hg), ("a", hg), ("cq", q_lora), ("ckv", kv_lora),
                        ("krot", rope), ("gates", 2 * d)):
        cols[name] = win[:, o:o + width]
        o += width
    assert o == win.shape[1]

    p = dict(n_gdn_heads=hg, dk=dk, dv=dv, qk_w=qk_w, v_w=v_w, n_mla_heads=hm, mla_qk=mla_qk, mla_v=mla_v,
             nope=nope, rope=rope, conv_k=conv_w.shape[1])
    p["n1g"] = norm1_g[0][None, :]
    p["wqkv"] = cols["qkv"].astype(BF16)
    p["convw"] = conv_w[0]
    head_of = np.arange(qk_w) // dk
    p["seg"] = jnp.asarray((head_of[:, None] == head_of[None, :]).astype(np.float32)).astype(BF16)
    p["wz"] = cols["z"].astype(BF16)
    wbg = jnp.zeros((d, LANES), F32).at[:, :hg].set(cols["b"]).at[:, hg:2 * hg].set(cols["a"])
    p["wbg"] = wbg.astype(BF16)
    nexp_a = jnp.zeros((LANES,), F32).at[hg:2 * hg].set(-jnp.exp(a_log[0]))
    dtb = jnp.zeros((LANES,), F32).at[hg:2 * hg].set(dt_bias[0])
    p["bgpar"] = jnp.stack([nexp_a, dtb])
    p["wcq"] = cols["cq"].astype(BF16)
    p["cqg"] = cq_norm_g[0][None, :]
    wuq = w_uq[0].reshape(q_lora, hm, mla_qk)
    p["wuq"] = jnp.pad(wuq, ((0, 0), (0, 0), (0, LANES - mla_qk))).reshape(q_lora, hm * LANES).astype(BF16)
    p["qng"] = jnp.pad(q_norm_g[0] * (mla_qk ** -0.5 * np.log2(np.e)), (0, LANES - mla_qk))[None, :]
    p["wckv"] = cols["ckv"].astype(BF16)
    p["ckvg"] = ckv_norm_g[0][None, :]
    wukv = w_ukv[0].reshape(kv_lora, hm, nope + mla_v)
    p["wuk"] = jnp.pad(wukv[:, :, :nope], ((0, 0), (0, 0), (0, LANES - nope))).reshape(kv_lora, hm * LANES).astype(BF16)
    p["wuvt"] = wukv[:, :, nope:].reshape(kv_lora, hm * mla_v).T.astype(BF16)
    p["wkrot"] = jnp.zeros((d, LANES), F32).at[:, nope:nope + rope].set(cols["krot"]).astype(BF16)
    p["kng"] = jnp.pad(k_norm_g[0], (0, LANES - mla_qk))[None, :]
    half = rope // 2
    inv = np.power(ROPE_THETA, -np.arange(half, dtype=np.float32) / half).astype(np.float32)
    invf = np.zeros((LANES,), np.float32)
    invf[nope:nope + half] = inv
    invf[nope + half:nope + rope] = inv
    sign = np.zeros((LANES,), np.float32)
    sign[nope:nope + half] = -1.0
    sign[nope + half:nope + rope] = 1.0
    p["ropepar"] = jnp.asarray(np.stack([invf, sign]))
    p["wg"] = cols["gates"].astype(BF16)

    p["gng"] = gdn_norm_g[0][None, :]
    p["wgo"] = w_gdn_out[0].astype(BF16)
    p["wmo"] = w_mla_out[0].astype(BF16)
    p["wo"] = w_o[0].astype(BF16)

    hp, _, nk, dkey = sub_keys.shape[1:]
    p["n_peer_heads"] = hp
    p["n_keys"] = nk
    assert nk == LANES and dkey == LANES
    p["n2g"] = norm2_g[0][None, :]
    p["wpqt"] = w_pq[0].T.astype(BF16)
    p["keys"] = sub_keys[0].reshape(hp * 2 * nk, dkey).astype(BF16)
    p["u"] = u_tab[0].astype(BF16)
    p["vt"] = v_tab[0].T.astype(BF16)
    return p


def _block_sizes(b, s):
    return dict(
        inproj_tokens=min(256, s),
        gdn_tokens=min(CHUNK * GDN_GROUP, s),
        attn_block=min(512, s),
        merge_tokens=min(512, s),
        peer_tokens=min(512, b * s),
        peer_experts=2048,
    )


def kernel(x, positions, norm1_g, w_in, conv_w, a_log, dt_bias, gdn_norm_g, w_gdn_out, cq_norm_g, w_uq,
           ckv_norm_g, w_ukv, q_norm_g, k_norm_g, w_mla_out, w_o, norm2_g, w_pq, sub_keys, u_tab, v_tab):
    assert w_in.shape[0] == 1, "single-layer block"
    b, s, d = x.shape
    p = _prepare(norm1_g, w_in, conv_w, a_log, dt_bias, gdn_norm_g, w_gdn_out, cq_norm_g, w_uq, ckv_norm_g,
                 w_ukv, q_norm_g, k_norm_g, w_mla_out, w_o, norm2_g, w_pq, sub_keys, u_tab, v_tab)
    blk = _block_sizes(b, s)
    gq, gk, gv, sz, bgc, sg, qp, kp, vt = _stage_inproj(x, positions, p, tm=blk["inproj_tokens"])
    oa = _stage_gdn(gq, gk, gv, bgc, sz, p, b=b, s=s, ct=blk["gdn_tokens"])
    att = _stage_attn(qp, kp, vt, blk=blk["attn_block"])
    x1 = _stage_merge(x.reshape(b * s, d), oa, att.reshape(b * s, -1), sg, p, tm=blk["merge_tokens"])
    out = _stage_peer(x1, p, tt=blk["peer_tokens"], ec=blk["peer_experts"])
    return out.reshape(b, s, d)
```
